```python
import math
import jax, jax.numpy as jnp
from jax import lax
import numpy as np

D_MODEL = 1024
BATCH = 2
SEQ = 16384
DEPTH = 4

CHUNK = 64
N_MIXERS = 3
DEEPNORM_ALPHA = (2 * DEPTH) ** 0.25
DEEPNORM_BETA = (8 * DEPTH) ** -0.25
LN_EPS = 1e-5
RMS_EPS = 1e-6
NEG_INF = -1e30

GLA_HEADS = 4
GLA_DK = D_MODEL // 2 // GLA_HEADS
GLA_DV = D_MODEL // GLA_HEADS
GLA_GATE_RANK = 16
GLA_TAU = 16.0
GLA_HK = GLA_HEADS * GLA_DK
GLA_HV = GLA_HEADS * GLA_DV

RPA_HEADS = 16
RPA_HEAD_DIM = D_MODEL // RPA_HEADS
RPA_LEFT_CHUNKS = 8
RPA_BAND = (RPA_LEFT_CHUNKS + 1) * CHUNK
RPA_MAX_REL = 128

MLA_HEADS = 8
MLA_NOPE = 128
MLA_ROPE = 64
MLA_V = 128
MLA_Q_RANK = 384
MLA_KV_RANK = 256
MLA_QK = MLA_NOPE + MLA_ROPE
ROPE_BASE = 10000.0
Q_BLOCK = 128

FFN_HIDDEN = -(-8 * D_MODEL // (3 * 256)) * 256

N_GLA = len(range(0, DEPTH, N_MIXERS))
N_RPA = len(range(1, DEPTH, N_MIXERS))
N_MLA = len(range(2, DEPTH, N_MIXERS))

kernel_name = "hybrid_gla_chunkrel_mla_deepnorm"


def layer_norm(x, g, b):
    xf = x.astype(jnp.float32)
    mu = jnp.mean(xf, -1, keepdims=True)
    var = jnp.mean(jnp.square(xf - mu), -1, keepdims=True)
    y = (xf - mu) * lax.rsqrt(var + LN_EPS)
    return (y * g.astype(jnp.float32) + b.astype(jnp.float32)).astype(x.dtype)


def rms_norm(x, g):
    xf = x.astype(jnp.float32)
    y = xf * lax.rsqrt(jnp.mean(jnp.square(xf), -1, keepdims=True) + RMS_EPS)
    return (y * g.astype(jnp.float32)).astype(x.dtype)


def gla_mixer(x, w_in, w_gate_up, b_gate, norm_g, w_out):
    B, S, _ = x.shape
    nc = S // CHUNK
    f32 = jnp.float32
    q, k, v, r, g_low = jnp.split(
        x @ w_in, [GLA_HK, 2 * GLA_HK, 2 * GLA_HK + GLA_HV, 2 * GLA_HK + 2 * GLA_HV], axis=-1)
    log_a = jax.nn.log_sigmoid((g_low @ w_gate_up + b_gate).astype(f32)) / GLA_TAU

    def to_chunks(t, d):
        return t.astype(f32).reshape(B, nc, CHUNK, GLA_HEADS, d).transpose(0, 3, 1, 2, 4)

    q = to_chunks(q, GLA_DK) * (GLA_DK ** -0.5)
    k = to_chunks(k, GLA_DK)
    v = to_chunks(v, GLA_DV)
    cum = jnp.cumsum(to_chunks(log_a, GLA_DK), axis=3)
    total = cum[:, :, :, -1:, :]
    q_dec = q * jnp.exp(cum)
    k_inv = k * jnp.exp(-cum)
    k_end = k * jnp.exp(total - cum)
    causal = jnp.tril(jnp.ones((CHUNK, CHUNK), dtype=bool))
    a = jnp.where(causal, jnp.einsum('bhcid,bhcjd->bhcij', q_dec, k_inv), 0.0)
    o_intra = jnp.einsum('bhcij,bhcjv->bhciv', a, v)
    decay = jnp.swapaxes(jnp.exp(total), -1, -2)

    def step(state, inp):
        qd, ke, vc, dc = inp
        o = jnp.einsum('bhid,bhdv->bhiv', qd, state)
        state = state * dc + jnp.einsum('bhjd,bhjv->bhdv', ke, vc)
        return state, o

    state0 = jnp.zeros((B, GLA_HEADS, GLA_DK, GLA_DV), f32)
    xs = (jnp.moveaxis(q_dec, 2, 0), jnp.moveaxis(k_end, 2, 0),
          jnp.moveaxis(v, 2, 0), jnp.moveaxis(decay, 2, 0))
    _, o_inter = lax.scan(step, state0, xs)
    o = o_intra + jnp.moveaxis(o_inter, 0, 2)
    o = o.transpose(0, 2, 3, 1, 4).reshape(B, S, GLA_HEADS, GLA_DV)
    mu = jnp.mean(o, -1, keepdims=True)
    var = jnp.mean(jnp.square(o - mu), -1, keepdims=True)
    o = ((o - mu) * lax.rsqrt(var + LN_EPS)).reshape(B, S, GLA_HV) * norm_g.astype(f32)
    y = jax.nn.silu(r.astype(f32)) * o
    return y.astype(x.dtype) @ w_out


def rpa_mixer(x, w_in, rel_bias, w_out):
    B, S, _ = x.shape
    nc = S // CHUNK
    pad = RPA_LEFT_CHUNKS * CHUNK
    q, k, v = jnp.split(x @ w_in, 3, axis=-1)
    q = q.reshape(B, nc, CHUNK, RPA_HEADS, RPA_HEAD_DIM) * (RPA_HEAD_DIM ** -0.5)
    padw = ((0, 0), (pad, 0), (0, 0), (0, 0))
    k = jnp.pad(k.reshape(B, S, RPA_HEADS, RPA_HEAD_DIM), padw)
    v = jnp.pad(v.reshape(B, S, RPA_HEADS, RPA_HEAD_DIM), padw)
    band_pos = jnp.arange(RPA_BAND)
    rel = jnp.clip(pad + jnp.arange(CHUNK)[:, None] - band_pos[None, :],
                   -RPA_MAX_REL, RPA_MAX_REL) + RPA_MAX_REL
    bias = rel_bias[:, rel].astype(jnp.float32)

    def one_chunk(c):
        q_c = lax.dynamic_index_in_dim(q, c, axis=1, keepdims=False)
        k_c = lax.dynamic_slice_in_dim(k, c * CHUNK, RPA_BAND, axis=1)
        v_c = lax.dynamic_slice_in_dim(v, c * CHUNK, RPA_BAND, axis=1)
        s = jnp.einsum('bihd,bjhd->bhij', q_c, k_c).astype(jnp.float32) + bias
        valid = (c * CHUNK - pad + band_pos) >= 0
        p = jax.nn.softmax(jnp.where(valid, s, NEG_INF), axis=-1)
        return jnp.einsum('bhij,bjhd->bihd', p.astype(v_c.dtype), v_c)

    o = lax.map(one_chunk, jnp.arange(nc))
    o = o.transpose(1, 0, 2, 3, 4).reshape(B, S, RPA_HEADS * RPA_HEAD_DIM)
    return o @ w_out


def rope_tables(S):
    inv = ROPE_BASE ** (-jnp.arange(0, MLA_ROPE, 2, dtype=jnp.float32) / MLA_ROPE)
    ang = jnp.arange(S, dtype=jnp.float32)[:, None] * inv[None, :]
    return jnp.cos(ang), jnp.sin(ang)


def apply_rope(t, cos, sin):
    extra = t.ndim - 3
    c = cos.reshape(cos.shape[0], *([1] * extra), cos.shape[1])
    s = sin.reshape(sin.shape[0], *([1] * extra), sin.shape[1])
    t1, t2 = jnp.split(t.astype(jnp.float32), 2, axis=-1)
    return jnp.concatenate([t1 * c - t2 * s, t1 * s + t2 * c], -1).astype(t.dtype)


def mla_mixer(x, w_in, q_norm_g, kv_norm_g, w_uq, w_ukv, w_out):
    B, S, _ = x.shape
    c_q, c_kv, k_rope = jnp.split(x @ w_in, [MLA_Q_RANK, MLA_Q_RANK + MLA_KV_RANK], axis=-1)
    q = (rms_norm(c_q, q_norm_g) @ w_uq).reshape(B, S, MLA_HEADS, MLA_QK)
    kv = (rms_norm(c_kv, kv_norm_g) @ w_ukv).reshape(B, S, MLA_HEADS, MLA_NOPE + MLA_V)
    q_nope, q_rope = jnp.split(q, [MLA_NOPE], axis=-1)
    k_nope, v = jnp.split(kv, [MLA_NOPE], axis=-1)
    cos, sin = rope_tables(S)
    q_rope = apply_rope(q_rope, cos, sin)
    k_rope = apply_rope(k_rope, cos, sin)
    scale = MLA_QK ** -0.5
    nqb = S // Q_BLOCK
    qn_b = jnp.moveaxis((q_nope * scale).reshape(B, nqb, Q_BLOCK, MLA_HEADS, MLA_NOPE), 1, 0)
    qr_b = jnp.moveaxis((q_rope * scale).reshape(B, nqb, Q_BLOCK, MLA_HEADS, MLA_ROPE), 1, 0)
    k_chunk = jnp.arange(S) // CHUNK

    def one_block(args):
        qn, qr, i = args
        s = (jnp.einsum('bqhd,bkhd->bhqk', qn, k_nope)
             + jnp.einsum('bqhr,bkr->bhqk', qr, k_rope)).astype(jnp.float32)
        q_chunk = (i * Q_BLOCK + jnp.arange(Q_BLOCK)) // CHUNK
        mask = k_chunk[None, :] <= q_chunk[:, None]
        p = jax.nn.softmax(jnp.where(mask, s, NEG_INF), axis=-1)
        return jnp.einsum('bhqk,bkhd->bqhd', p.astype(v.dtype), v)

    o = lax.map(one_block, (qn_b, qr_b, jnp.arange(nqb)))
    o = jnp.moveaxis(o, 0, 1).reshape(B, S, MLA_HEADS * MLA_V)
    return o @ w_out


def swiglu(x, w_in, w_out):
    gate, up = jnp.split(x @ w_in, 2, axis=-1)
    return (jax.nn.silu(gate) * up) @ w_out


def _w(k, shape, fan_in, scale=1.0):
    return jax.random.normal(k, shape, jnp.float32) * (scale * fan_in ** -0.5)


def setup_inputs(seed: int = 0) -> dict:
    key = jax.random.key(seed)
    ks = jax.random.split(key, 20)
    D = D_MODEL
    gla_cols = 2 * GLA_HK + 2 * GLA_HV + GLA_GATE_RANK
    mla_cols = MLA_Q_RANK + MLA_KV_RANK + MLA_ROPE
    return {
        "x": jax.random.normal(ks[0], (BATCH, SEQ, D), jnp.float32),
        "gla_w_in": _w(ks[1], (N_GLA, D, gla_cols), D),
        "gla_w_gate_up": _w(ks[2], (N_GLA, GLA_GATE_RANK, GLA_HK), GLA_GATE_RANK),
        "gla_b_gate": 0.1 * jax.random.normal(ks[3], (N_GLA, GLA_HK), jnp.float32),
        "gla_norm_g": 1.0 + 0.02 * jax.random.normal(ks[4], (N_GLA, GLA_HV), jnp.float32),
        "gla_w_out": _w(ks[5], (N_GLA, GLA_HV, D), GLA_HV, DEEPNORM_BETA),
        "rpa_w_in": _w(ks[6], (N_RPA, D, 3 * D), D),
        "rpa_rel_bias": 0.5 * jax.random.normal(ks[7], (N_RPA, RPA_HEADS, 2 * RPA_MAX_REL + 1), jnp.float32),
        "rpa_w_out": _w(ks[8], (N_RPA, D, D), D, DEEPNORM_BETA),
        "mla_w_in": _w(ks[9], (N_MLA, D, mla_cols), D),
        "mla_q_norm_g": 1.0 + 0.02 * jax.random.normal(ks[10], (N_MLA, MLA_Q_RANK), jnp.float32),
        "mla_kv_norm_g": 1.0 + 0.02 * jax.random.normal(ks[11], (N_MLA, MLA_KV_RANK), jnp.float32),
        "mla_w_uq": _w(ks[12], (N_MLA, MLA_Q_RANK, MLA_HEADS * MLA_QK), MLA_Q_RANK),
        "mla_w_ukv": _w(ks[13], (N_MLA, MLA_KV_RANK, MLA_HEADS * (MLA_NOPE + MLA_V)), MLA_KV_RANK),
        "mla_w_out": _w(ks[14], (N_MLA, MLA_HEADS * MLA_V, D), MLA_HEADS * MLA_V, DEEPNORM_BETA),
        "ffn_w_in": _w(ks[15], (DEPTH, D, 2 * FFN_HIDDEN), D),
        "ffn_w_out": _w(ks[16], (DEPTH, FFN_HIDDEN, D), FFN_HIDDEN, DEEPNORM_BETA),
        "ln_g": 1.0 + 0.02 * jax.random.normal(ks[17], (DEPTH, 2, D), jnp.float32),
        "ln_b": 0.02 * jax.random.normal(ks[18], (DEPTH, 2, D), jnp.float32),
    }


def reference(x, gla_w_in, gla_w_gate_up, gla_b_gate, gla_norm_g, gla_w_out,
              rpa_w_in, rpa_rel_bias, rpa_w_out,
              mla_w_in, mla_q_norm_g, mla_kv_norm_g, mla_w_uq, mla_w_ukv, mla_w_out,
              ffn_w_in, ffn_w_out, ln_g, ln_b):
    h = x
    for i in range(DEPTH):
        m, j = i % N_MIXERS, i // N_MIXERS
        if m == 0:
            y = gla_mixer(h, gla_w_in[j], gla_w_gate_up[j], gla_b_gate[j], gla_norm_g[j], gla_w_out[j])
        elif m == 1:
            y = rpa_mixer(h, rpa_w_in[j], rpa_rel_bias[j], rpa_w_out[j])
        else:
            y = mla_mixer(h, mla_w_in[j], mla_q_norm_g[j], mla_kv_norm_g[j],
                          mla_w_uq[j], mla_w_ukv[j], mla_w_out[j])
        h = layer_norm(DEEPNORM_ALPHA * h + y, ln_g[i, 0], ln_b[i, 0])
        h = layer_norm(DEEPNORM_ALPHA * h + swiglu(h, ffn_w_in[i], ffn_w_out[i]), ln_g[i, 1], ln_b[i, 1])
    return h
```

```python
import functools

import jax
import jax.numpy as jnp
from jax import lax
from jax.experimental import pallas as pl
from jax.experimental.pallas import tpu as pltpu

F32 = jnp.float32
BF16 = jnp.bfloat16

DEPTH = 4
N_MIXERS = 3
CHUNK = 64
DEEPNORM_ALPHA = (2 * DEPTH) ** 0.25
LN_EPS = 1e-5
RMS_EPS = 1e-6
NEG_INF = -1e30

GLA_HEADS = 4
GLA_DK = 128
GLA_DV = 256
GLA_GATE_RANK = 16
GLA_TAU = 16.0
GLA_HK = GLA_HEADS * GLA_DK
GLA_HV = GLA_HEADS * GLA_DV

RPA_HEADS = 16
RPA_HEAD_DIM = 64
RPA_LEFT_CHUNKS = 8
RPA_MAX_REL = 128
RPA_GROUP = 2
RPA_GQ = RPA_GROUP * CHUNK
RPA_GK = (RPA_LEFT_CHUNKS + RPA_GROUP) * CHUNK
RPA_TQ = 512

MLA_HEADS = 8
MLA_NOPE = 128
MLA_ROPE = 64
MLA_V = 128
MLA_Q_RANK = 384
MLA_KV_RANK = 256
MLA_QK = MLA_NOPE + MLA_ROPE
ROPE_BASE = 10000.0
MLA_TQ = 512
MLA_TK = 512

LANES = 128
VMEM_LIMIT = 56 * 1024 * 1024

TOKEN_TILE = 512
FFN_CHUNK = 256


def _params(sem):
    return pltpu.CompilerParams(dimension_semantics=sem, vmem_limit_bytes=VMEM_LIMIT)


def _const_spec(shape):
    nd = len(shape)
    return pl.BlockSpec(shape, lambda *_: (0,) * nd, pipeline_mode=pl.Buffered(1))


def _layer_norm(x, g, b):
    mu = jnp.mean(x, axis=-1, keepdims=True)
    xc = x - mu
    var = jnp.mean(xc * xc, axis=-1, keepdims=True)
    return xc * lax.rsqrt(var + LN_EPS) * g + b


def _dot(a, b):
    return jnp.dot(a, b, preferred_element_type=F32)


def _dot_nt(a, b):
    return lax.dot_general(a, b, (((1,), (1,)), ((), ())), preferred_element_type=F32)


def _dot_tn(a, b):
    return lax.dot_general(a, b, (((0,), (0,)), ((), ())), preferred_element_type=F32)


def _post_kernel(o_ref, h_ref, wo_ref, ln_ref, wgu_ref, wdn_ref, out_ref, xb_ref, acc_ref):
    y = _dot(o_ref[...], wo_ref[...])
    h1 = _layer_norm(DEEPNORM_ALPHA * h_ref[...] + y, ln_ref[0:1, :], ln_ref[1:2, :])
    out_ref[...] = h1
    xb_ref[...] = h1.astype(BF16)
    acc_ref[...] = jnp.zeros_like(acc_ref)

    def body(c, carry):
        xb = xb_ref[...]
        g = _dot(xb, wgu_ref[0, c])
        u = _dot(xb, wgu_ref[1, c])
        a = (g * jax.nn.sigmoid(g) * u).astype(BF16)
        acc_ref[...] += _dot(a, wdn_ref[c])
        return carry

    lax.fori_loop(0, wdn_ref.shape[0], body, 0)
    out_ref[...] = _layer_norm(DEEPNORM_ALPHA * out_ref[...] + acc_ref[...],
                               ln_ref[2:3, :], ln_ref[3:4, :])


def _post(o, h, w_o, ln, w_gu, w_dn):
    m, d = h.shape
    tm = TOKEN_TILE
    row = lambda i: (i, 0)
    return pl.pallas_call(
        _post_kernel,
        grid=(m // tm,),
        in_specs=[
            pl.BlockSpec((tm, o.shape[1]), row),
            pl.BlockSpec((tm, d), row),
            _const_spec(w_o.shape),
            _const_spec(ln.shape),
            _const_spec(w_gu.shape),
            _const_spec(w_dn.shape),
        ],
        out_specs=pl.BlockSpec((tm, d), row),
        out_shape=jax.ShapeDtypeStruct((m, d), F32),
        scratch_shapes=[pltpu.VMEM((tm, d), BF16), pltpu.VMEM((tm, d), F32)],
        compiler_params=_params(("parallel",)),
        name="post_ffn",
    )(o, h, w_o, ln, w_gu, w_dn)


def _gla_proj_kernel(h_ref, wm_ref, wgl_ref, wgu_ref, bg_ref, qk_ref, v_ref, r_ref, la_ref):
    xb = h_ref[...].astype(BF16)
    qk_ref[...] = _dot(xb, wm_ref[:, 0:2 * GLA_HK])
    v_ref[...] = _dot(xb, wm_ref[:, 2 * GLA_HK:2 * GLA_HK + GLA_HV]).astype(BF16)
    r_ref[...] = _dot(xb, wm_ref[:, 2 * GLA_HK + GLA_HV:])
    g_low = _dot(xb, wgl_ref[...])
    z = _dot(g_low.astype(BF16), wgu_ref[...]) + bg_ref[...]
    log_sig = jnp.minimum(z, 0.0) - jnp.log1p(jnp.exp(-jnp.abs(z)))
    la_ref[...] = log_sig / GLA_TAU


def _gla_proj(h, w_main, w_gl, w_gu, b_gate):
    m, d = h.shape
    tm = TOKEN_TILE
    row = lambda i: (i, 0)
    return pl.pallas_call(
        _gla_proj_kernel,
        grid=(m // tm,),
        in_specs=[
            pl.BlockSpec((tm, d), row),
            _const_spec(w_main.shape),
            _const_spec(w_gl.shape),
            _const_spec(w_gu.shape),
            _const_spec(b_gate.shape),
        ],
        out_specs=[
            pl.BlockSpec((tm, 2 * GLA_HK), row),
            pl.BlockSpec((tm, GLA_HV), row),
            pl.BlockSpec((tm, GLA_HV), row),
            pl.BlockSpec((tm, GLA_HK), row),
        ],
        out_shape=[
            jax.ShapeDtypeStruct((m, 2 * GLA_HK), F32),
            jax.ShapeDtypeStruct((m, GLA_HV), BF16),
            jax.ShapeDtypeStruct((m, GLA_HV), F32),
            jax.ShapeDtypeStruct((m, GLA_HK), F32),
        ],
        compiler_params=_params(("parallel",)),
        name="gla_proj",
    )(h, w_main, w_gl, w_gu, b_gate)


def _split3_bf16(x):
    hi = x.astype(BF16)
    r1 = x - hi.astype(F32)
    mid = r1.astype(BF16)
    lo = (r1 - mid.astype(F32)).astype(BF16)
    return hi, mid, lo


def _gla_core_kernel(qk_ref, v_ref, r_ref, la_ref, g_ref, y_ref, state_ref, *, n_chunks):
    @pl.when(pl.program_id(1) == 0)
    def _():
        state_ref[...] = jnp.zeros_like(state_ref)

    row_i = lax.broadcasted_iota(jnp.int32, (CHUNK, CHUNK), 0)
    col_i = lax.broadcasted_iota(jnp.int32, (CHUNK, CHUNK), 1)
    causal = col_i <= row_i
    tri = causal.astype(BF16)

    for ci in range(n_chunks):
        rows = slice(ci * CHUNK, (ci + 1) * CHUNK)
        la = la_ref[0, rows, :]
        hi, mid, lo = _split3_bf16(la)
        cum = _dot(tri, hi) + _dot(tri, mid) + _dot(tri, lo)
        tot = cum[CHUNK - 1:CHUNK, :]
        q = qk_ref[0, rows, 0:GLA_HK] * (GLA_DK ** -0.5)
        k = qk_ref[0, rows, GLA_HK:2 * GLA_HK]
        q_dec = (q * jnp.exp(cum)).astype(BF16)
        k_inv = (k * jnp.exp(-cum)).astype(BF16)
        k_end = (k * jnp.exp(tot - cum)).astype(BF16)
        decay = jnp.exp(tot)
        for h in range(GLA_HEADS):
            kc = slice(h * GLA_DK, (h + 1) * GLA_DK)
            vc = slice(h * GLA_DV, (h + 1) * GLA_DV)
            qd, ki, ke = q_dec[:, kc], k_inv[:, kc], k_end[:, kc]
            vh = v_ref[0, rows, vc]
            st = state_ref[h]
            a = jnp.where(causal, _dot_nt(qd, ki), 0.0)
            o = _dot(a.astype(BF16), vh) + _dot_nt(qd, st.astype(BF16))
            state_ref[h] = st * decay[:, kc] + _dot_tn(vh, ke)
            mu = jnp.mean(o, axis=-1, keepdims=True)
            oc = o - mu
            var = jnp.mean(oc * oc, axis=-1, keepdims=True)
            on = oc * lax.rsqrt(var + LN_EPS) * g_ref[:, vc]
            r = r_ref[0, rows, vc]
            y_ref[0, rows, vc] = (r * jax.nn.sigmoid(r) * on).astype(BF16)


def _gla_core(qk, v, r, la, norm_g, *, block):
    b, s, _ = qk.shape
    blk = lambda bi, ti: (bi, ti, 0)
    return pl.pallas_call(
        functools.partial(_gla_core_kernel, n_chunks=block // CHUNK),
        grid=(b, s // block),
        in_specs=[
            pl.BlockSpec((1, block, 2 * GLA_HK), blk),
            pl.BlockSpec((1, block, GLA_HV), blk),
            pl.BlockSpec((1, block, GLA_HV), blk),
            pl.BlockSpec((1, block, GLA_HK), blk),
            pl.BlockSpec((1, GLA_HV), lambda bi, ti: (0, 0)),
        ],
        out_specs=pl.BlockSpec((1, block, GLA_HV), blk),
        out_shape=jax.ShapeDtypeStruct((b, s, GLA_HV), BF16),
        scratch_shapes=[pltpu.VMEM((GLA_HEADS, GLA_DV, GLA_DK), F32)],
        compiler_params=_params(("parallel", "arbitrary")),
        name="gla_core",
    )(qk, v, r, la, norm_g)


def _gla_layer(h, b, s, w_in, w_gate_up, b_gate, norm_g):
    d = h.shape[1]
    n_main = 2 * GLA_HK + 2 * GLA_HV
    w_main = w_in[:, :n_main].astype(BF16)
    w_gl = jnp.zeros((d, LANES), BF16).at[:, :GLA_GATE_RANK].set(w_in[:, n_main:].astype(BF16))
    w_gu = jnp.zeros((LANES, GLA_HK), BF16).at[:GLA_GATE_RANK, :].set(w_gate_up.astype(BF16))
    qk, v, r, la = _gla_proj(h, w_main, w_gl, w_gu, b_gate.reshape(1, GLA_HK))
    sh = lambda t: t.reshape(b, s, t.shape[-1])
    y = _gla_core(sh(qk), sh(v), sh(r), sh(la), norm_g.reshape(1, GLA_HV), block=256)
    return y.reshape(b * s, GLA_HV)


def _rpa_proj_kernel(h_ref, w_ref, q_ref, k_ref, v_ref):
    xb = h_ref[...].astype(BF16)
    d = q_ref.shape[1]
    q_ref[...] = (_dot(xb, w_ref[:, 0:d]) * (RPA_HEAD_DIM ** -0.5)).astype(BF16)
    k_ref[...] = _dot(xb, w_ref[:, d:2 * d]).astype(BF16)
    v_ref[...] = _dot(xb, w_ref[:, 2 * d:3 * d]).astype(BF16)


def _rpa_proj(h, w):
    m, d = h.shape
    tm = TOKEN_TILE
    row = lambda i: (i, 0)
    return pl.pallas_call(
        _rpa_proj_kernel,
        grid=(m // tm,),
        in_specs=[pl.BlockSpec((tm, d), row), _const_spec(w.shape)],
        out_specs=[pl.BlockSpec((tm, d), row)] * 3,
        out_shape=[jax.ShapeDtypeStruct((m, d), BF16)] * 3,
        compiler_params=_params(("parallel",)),
        name="rpa_proj",
    )(h, w)


def _rpa_core_kernel(q_ref, kp_ref, kc_ref, vp_ref, vc_ref, bias_ref, o_ref, kw_ref, vw_ref):
    t = pl.program_id(1)
    tq = q_ref.shape[1]
    kw_ref[0:tq, :] = kp_ref[0]
    kw_ref[tq:2 * tq, :] = kc_ref[0]
    vw_ref[0:tq, :] = vp_ref[0]
    vw_ref[tq:2 * tq, :] = vc_ref[0]

    lane = lax.broadcasted_iota(jnp.int32, (RPA_GQ, LANES), 1)
    low_half = lane < RPA_HEAD_DIM
    win_col = lax.broadcasted_iota(jnp.int32, (RPA_GQ, RPA_GK), 1)

    def group(g, carry):
        r0 = pl.multiple_of(g * RPA_GQ, RPA_GQ)
        before_start = jnp.logical_and(t == 0, win_col + r0 < tq)
        for p in range(RPA_HEADS // 2):
            lanes = slice(p * LANES, (p + 1) * LANES)
            q2 = q_ref[0, pl.ds(r0, RPA_GQ), lanes]
            k2 = kw_ref[pl.ds(r0, RPA_GK), lanes]
            v2 = vw_ref[pl.ds(r0, RPA_GK), lanes]
            outs = []
            for j in range(2):
                keep = low_half if j == 0 else jnp.logical_not(low_half)
                qm = jnp.where(keep, q2, jnp.zeros_like(q2))
                s = _dot_nt(qm, k2) + bias_ref[2 * p + j]
                s = jnp.where(before_start, NEG_INF, s)
                m = jnp.max(s, axis=-1, keepdims=True)
                e = jnp.exp(s - m)
                l = jnp.sum(e, axis=-1, keepdims=True)
                outs.append(_dot(e.astype(BF16), v2) * (1.0 / l))
            o_ref[0, pl.ds(r0, RPA_GQ), lanes] = jnp.where(low_half, outs[0], outs[1]).astype(BF16)
        return carry

    lax.fori_loop(0, tq // RPA_GQ, group, 0)


def _rpa_core(q, k, v, bias):
    b, s, d = q.shape
    tq = RPA_TQ
    cur = lambda bi, ti: (bi, ti, 0)
    prev = lambda bi, ti: (bi, jnp.maximum(ti - 1, 0), 0)
    return pl.pallas_call(
        _rpa_core_kernel,
        grid=(b, s // tq),
        in_specs=[
            pl.BlockSpec((1, tq, d), cur),
            pl.BlockSpec((1, tq, d), prev),
            pl.BlockSpec((1, tq, d), cur),
            pl.BlockSpec((1, tq, d), prev),
            pl.BlockSpec((1, tq, d), cur),
            _const_spec(bias.shape),
        ],
        out_specs=pl.BlockSpec((1, tq, d), cur),
        out_shape=jax.ShapeDtypeStruct((b, s, d), BF16),
        scratch_shapes=[pltpu.VMEM((2 * tq, d), BF16), pltpu.VMEM((2 * tq, d), BF16)],
        compiler_params=_params(("parallel", "parallel")),
        name="rpa_core",
    )(q, k, k, v, v, bias)


def _rpa_group_bias(rel_bias):
    r = jnp.arange(RPA_GQ)[:, None]
    w = jnp.arange(RPA_GK)[None, :]
    rel = jnp.clip(RPA_LEFT_CHUNKS * CHUNK + r - w, -RPA_MAX_REL, RPA_MAX_REL) + RPA_MAX_REL
    band = w - (r // CHUNK) * CHUNK
    in_band = jnp.logical_and(band >= 0, band < (RPA_LEFT_CHUNKS + 1) * CHUNK)
    return jnp.where(in_band[None], rel_bias[:, rel].astype(F32), NEG_INF)


def _rpa_layer(h, b, s, w_in, rel_bias):
    d = h.shape[1]
    q, k, v = _rpa_proj(h, w_in.astype(BF16))
    sh = lambda t: t.reshape(b, s, d)
    o = _rpa_core(sh(q), sh(k), sh(v), _rpa_group_bias(rel_bias))
    return o.reshape(b * s, d)


def _rms_norm(x, g):
    return x * lax.rsqrt(jnp.mean(x * x, axis=-1, keepdims=True) + RMS_EPS) * g


def _mla_proj_kernel(h_ref, cos_ref, sin_ref, wcq_ref, wckv_ref, wkr_ref, wkrs_ref, gq_ref, gkv_ref,
                     wqn_ref, wqr_ref, wqrs_ref, wkn_ref, wv_ref,
                     qn_ref, qr_ref, kn_ref, kr_ref, v_ref):
    xb = h_ref[...].astype(BF16)
    cos = cos_ref[...]
    sin = sin_ref[...]
    scale = MLA_QK ** -0.5
    cq = _rms_norm(_dot(xb, wcq_ref[...]), gq_ref[...]).astype(BF16)
    ckv = _rms_norm(_dot(xb, wckv_ref[...]), gkv_ref[...]).astype(BF16)
    qn_ref[...] = (_dot(cq, wqn_ref[...]) * scale).astype(BF16)
    cos_h = jnp.concatenate([cos] * MLA_HEADS, axis=1)
    sin_h = jnp.concatenate([sin] * MLA_HEADS, axis=1)
    qr = _dot(cq, wqr_ref[...]) * cos_h + _dot(cq, wqrs_ref[...]) * sin_h
    qr_ref[...] = (qr * scale).astype(BF16)
    kn_ref[...] = _dot(ckv, wkn_ref[...]).astype(BF16)
    v_ref[...] = _dot(ckv, wv_ref[...]).astype(BF16)
    kr = _dot(xb, wkr_ref[...]) * cos + _dot(xb, wkrs_ref[...]) * sin
    kr_ref[...] = kr.astype(BF16)


def _mla_proj(h, s, cos, sin, weights):
    m, d = h.shape
    tm = TOKEN_TILE
    row = lambda i: (i, 0)
    pos = lambda i: (i % (s // tm), 0)
    wide = MLA_HEADS * LANES
    return pl.pallas_call(
        _mla_proj_kernel,
        grid=(m // tm,),
        in_specs=[pl.BlockSpec((tm, d), row), pl.BlockSpec((tm, LANES), pos), pl.BlockSpec((tm, LANES), pos)]
        + [_const_spec(w.shape) for w in weights],
        out_specs=[pl.BlockSpec((tm, wide), row)] * 3 + [pl.BlockSpec((tm, LANES), row), pl.BlockSpec((tm, wide), row)],
        out_shape=[jax.ShapeDtypeStruct((m, wide), BF16)] * 3
        + [jax.ShapeDtypeStruct((m, LANES), BF16), jax.ShapeDtypeStruct((m, wide), BF16)],
        compiler_params=_params(("parallel",)),
        name="mla_proj",
    )(h, cos, sin, *weights)


def _mla_attn_kernel(qn_ref, qr_ref, kn_ref, kr_ref, v_ref, o_ref, qc_ref, m_ref, l_ref, acc_ref):
    qi = pl.program_id(2)
    tq, tk = MLA_TQ, MLA_TK
    qc_ref[...] = jnp.concatenate([qn_ref[0], qr_ref[0]], axis=1)
    m_ref[...] = jnp.full_like(m_ref, NEG_INF)
    l_ref[...] = jnp.zeros_like(l_ref)
    acc_ref[...] = jnp.zeros_like(acc_ref)

    def step(j, masked):
        rows = pl.ds(pl.multiple_of(j * tk, tk), tk)
        kcat = jnp.concatenate([kn_ref[0, rows, :], kr_ref[0, rows, :]], axis=1)
        s = _dot_nt(qc_ref[...], kcat)
        if masked:
            qchunk = lax.broadcasted_iota(jnp.int32, (tq, tk), 0) // CHUNK
            kchunk = lax.broadcasted_iota(jnp.int32, (tq, tk), 1) // CHUNK
            s = jnp.where(kchunk <= qchunk, s, NEG_INF)
        m_prev = m_ref[...]
        m_new = jnp.maximum(m_prev, jnp.max(s, axis=-1, keepdims=True))
        alpha = jnp.exp(m_prev - m_new)
        p = jnp.exp(s - m_new)
        l_ref[...] = alpha * l_ref[...] + jnp.sum(p, axis=-1, keepdims=True)
        acc_ref[...] = alpha * acc_ref[...] + _dot(p.astype(BF16), v_ref[0, rows, :])
        m_ref[...] = m_new

    def body(j, carry):
        step(j, masked=False)
        return carry

    lax.fori_loop(0, qi, body, 0)
    step(qi, masked=True)
    o_ref[0] = (acc_ref[...] * (1.0 / l_ref[...])).astype(BF16)


def _mla_attn(qn, qr, kn, kr, v):
    b, s, _ = qn.shape
    tq = MLA_TQ
    assert MLA_TQ == MLA_TK
    qmap = lambda bi, hi, qi: (bi, qi, hi)
    kmap = lambda bi, hi, qi: (bi, 0, hi)
    return pl.pallas_call(
        _mla_attn_kernel,
        grid=(b, MLA_HEADS, s // tq),
        in_specs=[
            pl.BlockSpec((1, tq, LANES), qmap),
            pl.BlockSpec((1, tq, LANES), qmap),
            pl.BlockSpec((1, s, LANES), kmap),
            pl.BlockSpec((1, s, LANES), lambda bi, hi, qi: (bi, 0, 0)),
            pl.BlockSpec((1, s, LANES), kmap),
        ],
        out_specs=pl.BlockSpec((1, tq, LANES), qmap),
        out_shape=jax.ShapeDtypeStruct((b, s, MLA_HEADS * MLA_V), BF16),
        scratch_shapes=[
            pltpu.VMEM((tq, 2 * LANES), BF16),
            pltpu.VMEM((tq, 1), F32),
            pltpu.VMEM((tq, 1), F32),
            pltpu.VMEM((tq, MLA_V), F32),
        ],
        compiler_params=_params(("parallel", "parallel", "arbitrary")),
        name="mla_attn",
    )(qn, qr, kn, kr, v)


def _swap_halves(w):
    half = w.shape[-1] // 2
    return jnp.concatenate([w[..., half:], w[..., :half]], axis=-1)


def _mla_layer(h, b, s, w_in, q_norm_g, kv_norm_g, w_uq, w_ukv):
    d = h.shape[1]
    pad = LANES - MLA_ROPE
    w_cq = w_in[:, :MLA_Q_RANK]
    w_ckv = w_in[:, MLA_Q_RANK:MLA_Q_RANK + MLA_KV_RANK]
    w_kr = w_in[:, MLA_Q_RANK + MLA_KV_RANK:]
    pad_cols = lambda w: jnp.pad(w, [(0, 0)] * (w.ndim - 1) + [(0, pad)])
    w_uq = w_uq.reshape(MLA_Q_RANK, MLA_HEADS, MLA_QK)
    w_qn = w_uq[:, :, :MLA_NOPE].reshape(MLA_Q_RANK, MLA_HEADS * MLA_NOPE)
    w_qr = w_uq[:, :, MLA_NOPE:]
    w_qr_pad = pad_cols(w_qr).reshape(MLA_Q_RANK, MLA_HEADS * LANES)
    w_qrs_pad = pad_cols(_swap_halves(w_qr)).reshape(MLA_Q_RANK, MLA_HEADS * LANES)
    w_ukv = w_ukv.reshape(MLA_KV_RANK, MLA_HEADS, MLA_NOPE + MLA_V)
    w_kn = w_ukv[:, :, :MLA_NOPE].reshape(MLA_KV_RANK, MLA_HEADS * MLA_NOPE)
    w_v = w_ukv[:, :, MLA_NOPE:].reshape(MLA_KV_RANK, MLA_HEADS * MLA_V)
    inv = ROPE_BASE ** (-jnp.arange(0, MLA_ROPE, 2, dtype=F32) / MLA_ROPE)
    ang = jnp.arange(s, dtype=F32)[:, None] * inv[None, :]
    cos, sin = jnp.cos(ang), jnp.sin(ang)
    cos_t = jnp.concatenate([cos, cos, cos, cos], axis=1)
    sin_t = jnp.concatenate([-sin, sin, -sin, sin], axis=1)
    bf = lambda w: w.astype(BF16)
    weights = [bf(w_cq), bf(w_ckv), bf(pad_cols(w_kr)), bf(pad_cols(_swap_halves(w_kr))),
               q_norm_g.reshape(1, -1), kv_norm_g.reshape(1, -1),
               bf(w_qn), bf(w_qr_pad), bf(w_qrs_pad), bf(w_kn), bf(w_v)]
    qn, qr, kn, kr, v = _mla_proj(h, s, cos_t, sin_t, weights)
    sh = lambda t: t.reshape(b, s, t.shape[-1])
    o = _mla_attn(sh(qn), sh(qr), sh(kn), sh(kr), sh(v))
    return o.reshape(b * s, MLA_HEADS * MLA_V)


def kernel(x, gla_w_in, gla_w_gate_up, gla_b_gate, gla_norm_g, gla_w_out,
           rpa_w_in, rpa_rel_bias, rpa_w_out,
           mla_w_in, mla_q_norm_g, mla_kv_norm_g, mla_w_uq, mla_w_ukv, mla_w_out,
           ffn_w_in, ffn_w_out, ln_g, ln_b):
    b, s, d = x.shape
    h = x.reshape(b * s, d)
    hidden = ffn_w_out.shape[1]
    n_ch = hidden // FFN_CHUNK
    for i in range(DEPTH):
        m, j = i % N_MIXERS, i // N_MIXERS
        if m == 0:
            o = _gla_layer(h, b, s, gla_w_in[j], gla_w_gate_up[j], gla_b_gate[j], gla_norm_g[j])
            w_o = gla_w_out[j]
        elif m == 1:
            o = _rpa_layer(h, b, s, rpa_w_in[j], rpa_rel_bias[j])
            w_o = rpa_w_out[j]
        else:
            o = _mla_layer(h, b, s, mla_w_in[j], mla_q_norm_g[j], mla_kv_norm_g[j], mla_w_uq[j], mla_w_ukv[j])
            w_o = mla_w_out[j]
        w_gu = ffn_w_in[i].astype(BF16).reshape(d, 2, n_ch, FFN_CHUNK).transpose(1, 2, 0, 3)
        w_dn = ffn_w_out[i].astype(BF16).reshape(n_ch, FFN_CHUNK, d)
        ln = jnp.stack([ln_g[i, 0], ln_b[i, 0], ln_g[i, 1], ln_b[i, 1]])
        h = _post(o, h, w_o.astype(BF16), ln, w_gu, w_dn)
    return h.reshape(b, s, d)
```

```python
import functools

import jax
import jax.numpy as jnp
from jax import lax
from jax.experimental import pallas as pl
from jax.experimental.pallas import tpu as pltpu

F32 = jnp.float32
BF16 = jnp.bfloat16

DEPTH = 4
N_MIXERS = 3
CHUNK = 64
DEEPNORM_ALPHA = (2 * DEPTH) ** 0.25
LN_EPS = 1e-5
RMS_EPS = 1e-6
NEG_INF = -1e30

GLA_HEADS = 4
GLA_DK = 128
GLA_DV = 256
GLA_GATE_RANK = 16
GLA_TAU = 16.0
GLA_HK = GLA_HEADS * GLA_DK
GLA_HV = GLA_HEADS * GLA_DV

RPA_HEADS = 16
RPA_HEAD_DIM = 64
RPA_LEFT_CHUNKS = 8
RPA_MAX_REL = 128
RPA_GROUP = 2
RPA_GQ = RPA_GROUP * CHUNK
RPA_GK = (RPA_LEFT_CHUNKS + RPA_GROUP) * CHUNK
RPA_TQ = 512

MLA_HEADS = 8
MLA_NOPE = 128
MLA_ROPE = 64
MLA_V = 128
MLA_Q_RANK = 384
MLA_KV_RANK = 256
MLA_QK = MLA_NOPE + MLA_ROPE
ROPE_BASE = 10000.0
MLA_TQ = 512
MLA_TK = 512

LANES = 128
VMEM_LIMIT = 56 * 1024 * 1024

TOKEN_TILE = 512
FFN_CHUNK = 256


def _params(sem):
    return pltpu.CompilerParams(dimension_semantics=sem, vmem_limit_bytes=VMEM_LIMIT)


def _const_spec(shape):
    nd = len(shape)
    return pl.BlockSpec(shape, lambda *_: (0,) * nd, pipeline_mode=pl.Buffered(1))


def _layer_norm(x, g, b):
    mu = jnp.mean(x, axis=-1, keepdims=True)
    xc = x - mu
    var = jnp.mean(xc * xc, axis=-1, keepdims=True)
    return xc * lax.rsqrt(var + LN_EPS) * g + b


def _dot(a, b):
    return jnp.dot(a, b, preferred_element_type=F32)


def _dot_nt(a, b):
    return lax.dot_general(a, b, (((1,), (1,)), ((), ())), preferred_element_type=F32)


def _dot_tn(a, b):
    return lax.dot_general(a, b, (((0,), (0,)), ((), ())), preferred_element_type=F32)


def _post_kernel(o_ref, h_ref, wo_ref, ln_ref, wgu_ref, wdn_ref, out_ref, xb_ref, acc_ref):
    y = _dot(o_ref[...], wo_ref[...])
    h1 = _layer_norm(DEEPNORM_ALPHA * h_ref[...] + y, ln_ref[0:1, :], ln_ref[1:2, :])
    out_ref[...] = h1
    xb_ref[...] = h1.astype(BF16)
    acc_ref[...] = jnp.zeros_like(acc_ref)

    def body(c, carry):
        xb = xb_ref[...]
        g = _dot(xb, wgu_ref[0, c])
        u = _dot(xb, wgu_ref[1, c])
        a = (g * jax.nn.sigmoid(g) * u).astype(BF16)
        acc_ref[...] += _dot(a, wdn_ref[c])
        return carry

    lax.fori_loop(0, wdn_ref.shape[0], body, 0, unroll=True)
    out_ref[...] = _layer_norm(DEEPNORM_ALPHA * out_ref[...] + acc_ref[...],
                               ln_ref[2:3, :], ln_ref[3:4, :])


def _post(o, h, w_o, ln, w_gu, w_dn):
    m, d = h.shape
    tm = TOKEN_TILE
    row = lambda i: (i, 0)
    return pl.pallas_call(
        _post_kernel,
        grid=(m // tm,),
        in_specs=[
            pl.BlockSpec((tm, o.shape[1]), row),
            pl.BlockSpec((tm, d), row),
            _const_spec(w_o.shape),
            _const_spec(ln.shape),
            _const_spec(w_gu.shape),
            _const_spec(w_dn.shape),
        ],
        out_specs=pl.BlockSpec((tm, d), row),
        out_shape=jax.ShapeDtypeStruct((m, d), F32),
        scratch_shapes=[pltpu.VMEM((tm, d), BF16), pltpu.VMEM((tm, d), F32)],
        compiler_params=_params(("parallel",)),
        name="post_ffn",
    )(o, h, w_o, ln, w_gu, w_dn)


def _gla_proj_kernel(h_ref, wm_ref, wgl_ref, wgu_ref, bg_ref, qk_ref, v_ref, r_ref, la_ref):
    xb = h_ref[...].astype(BF16)
    qk_ref[...] = _dot(xb, wm_ref[:, 0:2 * GLA_HK])
    v_ref[...] = _dot(xb, wm_ref[:, 2 * GLA_HK:2 * GLA_HK + GLA_HV]).astype(BF16)
    r_ref[...] = _dot(xb, wm_ref[:, 2 * GLA_HK + GLA_HV:])
    g_low = _dot(xb, wgl_ref[...])
    z = _dot(g_low.astype(BF16), wgu_ref[...]) + bg_ref[...]
    log_sig = jnp.minimum(z, 0.0) - jnp.log1p(jnp.exp(-jnp.abs(z)))
    la_ref[...] = log_sig / GLA_TAU


def _gla_proj(h, w_main, w_gl, w_gu, b_gate):
    m, d = h.shape
    tm = TOKEN_TILE
    row = lambda i: (i, 0)
    return pl.pallas_call(
        _gla_proj_kernel,
        grid=(m // tm,),
        in_specs=[
            pl.BlockSpec((tm, d), row),
            _const_spec(w_main.shape),
            _const_spec(w_gl.shape),
            _const_spec(w_gu.shape),
            _const_spec(b_gate.shape),
        ],
        out_specs=[
            pl.BlockSpec((tm, 2 * GLA_HK), row),
            pl.BlockSpec((tm, GLA_HV), row),
            pl.BlockSpec((tm, GLA_HV), row),
            pl.BlockSpec((tm, GLA_HK), row),
        ],
        out_shape=[
            jax.ShapeDtypeStruct((m, 2 * GLA_HK), F32),
            jax.ShapeDtypeStruct((m, GLA_HV), BF16),
            jax.ShapeDtypeStruct((m, GLA_HV), F32),
            jax.ShapeDtypeStruct((m, GLA_HK), F32),
        ],
        compiler_params=_params(("parallel",)),
        name="gla_proj",
    )(h, w_main, w_gl, w_gu, b_gate)


def _split3_bf16(x):
    hi = x.astype(BF16)
    r1 = x - hi.astype(F32)
    mid = r1.astype(BF16)
    lo = (r1 - mid.astype(F32)).astype(BF16)
    return hi, mid, lo


def _gla_core_kernel(qk_ref, v_ref, r_ref, la_ref, g_ref, y_ref, state_ref, *, n_chunks):
    @pl.when(pl.program_id(1) == 0)
    def _():
        state_ref[...] = jnp.zeros_like(state_ref)

    row_i = lax.broadcasted_iota(jnp.int32, (CHUNK, CHUNK), 0)
    col_i = lax.broadcasted_iota(jnp.int32, (CHUNK, CHUNK), 1)
    causal = col_i <= row_i
    tri = causal.astype(BF16)

    for ci in range(n_chunks):
        rows = slice(ci * CHUNK, (ci + 1) * CHUNK)
        la = la_ref[0, rows, :]
        hi, mid, lo = _split3_bf16(la)
        cum = _dot(tri, hi) + _dot(tri, mid) + _dot(tri, lo)
        tot = cum[CHUNK - 1:CHUNK, :]
        q = qk_ref[0, rows, 0:GLA_HK] * (GLA_DK ** -0.5)
        k = qk_ref[0, rows, GLA_HK:2 * GLA_HK]
        q_dec = (q * jnp.exp(cum)).astype(BF16)
        k_inv = (k * jnp.exp(-cum)).astype(BF16)
        k_end = (k * jnp.exp(tot - cum)).astype(BF16)
        decay = jnp.exp(tot)
        for h in range(GLA_HEADS):
            kc = slice(h * GLA_DK, (h + 1) * GLA_DK)
            vc = slice(h * GLA_DV, (h + 1) * GLA_DV)
            qd, ki, ke = q_dec[:, kc], k_inv[:, kc], k_end[:, kc]
            vh = v_ref[0, rows, vc]
            st = state_ref[h]
            a = jnp.where(causal, _dot_nt(qd, ki), 0.0)
            o = _dot(a.astype(BF16), vh) + _dot_nt(qd, st.astype(BF16))
            state_ref[h] = st * decay[:, kc] + _dot_tn(vh, ke)
            mu = jnp.mean(o, axis=-1, keepdims=True)
            oc = o - mu
            var = jnp.mean(oc * oc, axis=-1, keepdims=True)
            on = oc * lax.rsqrt(var + LN_EPS) * g_ref[:, vc]
            r = r_ref[0, rows, vc]
            y_ref[0, rows, vc] = (r * jax.nn.sigmoid(r) * on).astype(BF16)


def _gla_core(qk, v, r, la, norm_g, *, block):
    b, s, _ = qk.shape
    blk = lambda bi, ti: (bi, ti, 0)
    return pl.pallas_call(
        functools.partial(_gla_core_kernel, n_chunks=block // CHUNK),
        grid=(b, s // block),
        in_specs=[
            pl.BlockSpec((1, block, 2 * GLA_HK), blk),
            pl.BlockSpec((1, block, GLA_HV), blk),
            pl.BlockSpec((1, block, GLA_HV), blk),
            pl.BlockSpec((1, block, GLA_HK), blk),
            pl.BlockSpec((1, GLA_HV), lambda bi, ti: (0, 0)),
        ],
        out_specs=pl.BlockSpec((1, block, GLA_HV), blk),
        out_shape=jax.ShapeDtypeStruct((b, s, GLA_HV), BF16),
        scratch_shapes=[pltpu.VMEM((GLA_HEADS, GLA_DV, GLA_DK), F32)],
        compiler_params=_params(("parallel", "arbitrary")),
        name="gla_core",
    )(qk, v, r, la, norm_g)


def _gla_layer(h, b, s, w_in, w_gate_up, b_gate, norm_g):
    d = h.shape[1]
    n_main = 2 * GLA_HK + 2 * GLA_HV
    w_main = w_in[:, :n_main].astype(BF16)
    w_gl = jnp.zeros((d, LANES), BF16).at[:, :GLA_GATE_RANK].set(w_in[:, n_main:].astype(BF16))
    w_gu = jnp.zeros((LANES, GLA_HK), BF16).at[:GLA_GATE_RANK, :].set(w_gate_up.astype(BF16))
    qk, v, r, la = _gla_proj(h, w_main, w_gl, w_gu, b_gate.reshape(1, GLA_HK))
    sh = lambda t: t.reshape(b, s, t.shape[-1])
    y = _gla_core(sh(qk), sh(v), sh(r), sh(la), norm_g.reshape(1, GLA_HV), block=256)
    return y.reshape(b * s, GLA_HV)


def _rpa_proj_kernel(h_ref, w_ref, q_ref, k_ref, v_ref):
    xb = h_ref[...].astype(BF16)
    d = q_ref.shape[1]
    q_ref[...] = (_dot(xb, w_ref[:, 0:d]) * (RPA_HEAD_DIM ** -0.5)).astype(BF16)
    k_ref[...] = _dot(xb, w_ref[:, d:2 * d]).astype(BF16)
    v_ref[...] = _dot(xb, w_ref[:, 2 * d:3 * d]).astype(BF16)


def _rpa_proj(h, w):
    m, d = h.shape
    tm = TOKEN_TILE
    row = lambda i: (i, 0)
    return pl.pallas_call(
        _rpa_proj_kernel,
        grid=(m // tm,),
        in_specs=[pl.BlockSpec((tm, d), row), _const_spec(w.shape)],
        out_specs=[pl.BlockSpec((tm, d), row)] * 3,
        out_shape=[jax.ShapeDtypeStruct((m, d), BF16)] * 3,
        compiler_params=_params(("parallel",)),
        name="rpa_proj",
    )(h, w)


def _rpa_core_kernel(q_ref, kp_ref, kc_ref, vp_ref, vc_ref, bias_ref, o_ref, kw_ref, vw_ref):
    t = pl.program_id(1)
    tq = q_ref.shape[1]
    kw_ref[0:tq, :] = kp_ref[0]
    kw_ref[tq:2 * tq, :] = kc_ref[0]
    vw_ref[0:tq, :] = vp_ref[0]
    vw_ref[tq:2 * tq, :] = vc_ref[0]

    lane = lax.broadcasted_iota(jnp.int32, (RPA_GQ, LANES), 1)
    low_half = lane < RPA_HEAD_DIM
    win_col = lax.broadcasted_iota(jnp.int32, (RPA_GQ, RPA_GK), 1)

    def group(g, carry):
        r0 = pl.multiple_of(g * RPA_GQ, RPA_GQ)
        before_start = jnp.logical_and(t == 0, win_col + r0 < tq)
        for p in range(RPA_HEADS // 2):
            lanes = slice(p * LANES, (p + 1) * LANES)
            q2 = q_ref[0, pl.ds(r0, RPA_GQ), lanes]
            k2 = kw_ref[pl.ds(r0, RPA_GK), lanes]
            v2 = vw_ref[pl.ds(r0, RPA_GK), lanes]
            outs = []
            for j in range(2):
                keep = low_half if j == 0 else jnp.logical_not(low_half)
                qm = jnp.where(keep, q2, jnp.zeros_like(q2))
                s = _dot_nt(qm, k2) + bias_ref[2 * p + j]
                s = jnp.where(before_start, NEG_INF, s)
                m = jnp.max(s, axis=-1, keepdims=True)
                e = jnp.exp(s - m)
                l = jnp.sum(e, axis=-1, keepdims=True)
                outs.append(_dot(e.astype(BF16), v2) * (1.0 / l))
            o_ref[0, pl.ds(r0, RPA_GQ), lanes] = jnp.where(low_half, outs[0], outs[1]).astype(BF16)
        return carry

    lax.fori_loop(0, tq // RPA_GQ, group, 0)


def _rpa_core(q, k, v, bias):
    b, s, d = q.shape
    tq = RPA_TQ
    cur = lambda bi, ti: (bi, ti, 0)
    prev = lambda bi, ti: (bi, jnp.maximum(ti - 1, 0), 0)
    return pl.pallas_call(
        _rpa_core_kernel,
        grid=(b, s // tq),
        in_specs=[
            pl.BlockSpec((1, tq, d), cur),
            pl.BlockSpec((1, tq, d), prev),
            pl.BlockSpec((1, tq, d), cur),
            pl.BlockSpec((1, tq, d), prev),
            pl.BlockSpec((1, tq, d), cur),
            _const_spec(bias.shape),
        ],
        out_specs=pl.BlockSpec((1, tq, d), cur),
        out_shape=jax.ShapeDtypeStruct((b, s, d), BF16),
        scratch_shapes=[pltpu.VMEM((2 * tq, d), BF16), pltpu.VMEM((2 * tq, d), BF16)],
        compiler_params=_params(("parallel", "parallel")),
        name="rpa_core",
    )(q, k, k, v, v, bias)


def _rpa_group_bias(rel_bias):
    r = jnp.arange(RPA_GQ)[:, None]
    w = jnp.arange(RPA_GK)[None, :]
    rel = jnp.clip(RPA_LEFT_CHUNKS * CHUNK + r - w, -RPA_MAX_REL, RPA_MAX_REL) + RPA_MAX_REL
    band = w - (r // CHUNK) * CHUNK
    in_band = jnp.logical_and(band >= 0, band < (RPA_LEFT_CHUNKS + 1) * CHUNK)
    return jnp.where(in_band[None], rel_bias[:, rel].astype(F32), NEG_INF)


def _rpa_layer(h, b, s, w_in, rel_bias):
    d = h.shape[1]
    q, k, v = _rpa_proj(h, w_in.astype(BF16))
    sh = lambda t: t.reshape(b, s, d)
    o = _rpa_core(sh(q), sh(k), sh(v), _rpa_group_bias(rel_bias))
    return o.reshape(b * s, d)


def _rms_norm(x, g):
    return x * lax.rsqrt(jnp.mean(x * x, axis=-1, keepdims=True) + RMS_EPS) * g


def _mla_proj_kernel(h_ref, cos_ref, sin_ref, cost_ref, sint_ref, wcq_ref, wckv_ref, wkr_ref, wkrs_ref,
                     gq_ref, gkv_ref, wqt_ref, wqst_ref, wkn_ref, wvt_ref,
                     qt_ref, kn_ref, kr_ref, vt_ref):
    xb = h_ref[...].astype(BF16)
    scale = MLA_QK ** -0.5
    cq = _rms_norm(_dot(xb, wcq_ref[...]), gq_ref[...]).astype(BF16)
    ckv = _rms_norm(_dot(xb, wckv_ref[...]), gkv_ref[...]).astype(BF16)
    kn_ref[...] = _dot(ckv, wkn_ref[...]).astype(BF16)
    kr = _dot(xb, wkr_ref[...]) * cos_ref[...] + _dot(xb, wkrs_ref[...]) * sin_ref[...]
    kr_ref[...] = kr.astype(BF16)
    vt_ref[0, 0] = _dot_nt(wvt_ref[...], ckv).astype(BF16)
    a = _dot_nt(wqt_ref[...], cq)
    bsw = _dot_nt(wqst_ref[...], cq)
    cos_t = cost_ref[...]
    sin_t = sint_ref[...]
    hq = 2 * LANES
    for h in range(MLA_HEADS):
        r0 = h * hq
        qt_ref[0, 0, r0:r0 + MLA_NOPE, :] = (a[r0:r0 + MLA_NOPE] * scale).astype(BF16)
        rope = a[r0 + MLA_NOPE:r0 + MLA_QK] * cos_t + bsw[h * MLA_ROPE:(h + 1) * MLA_ROPE] * sin_t
        qt_ref[0, 0, r0 + MLA_NOPE:r0 + MLA_QK, :] = (rope * scale).astype(BF16)
        qt_ref[0, 0, r0 + MLA_QK:r0 + hq, :] = jnp.zeros((hq - MLA_QK, qt_ref.shape[3]), BF16)


def _mla_proj(h, b, s, tables, weights):
    m, d = h.shape
    tm = MLA_TQ
    nt = s // tm
    row = lambda i: (i, 0)
    pos = lambda i: (i % nt, 0)
    pos_t = lambda i: (0, i % nt)
    tile_t = lambda i: (i // nt, i % nt, 0, 0)
    wide = MLA_HEADS * LANES
    return pl.pallas_call(
        _mla_proj_kernel,
        grid=(m // tm,),
        in_specs=[pl.BlockSpec((tm, d), row), pl.BlockSpec((tm, LANES), pos), pl.BlockSpec((tm, LANES), pos),
                  pl.BlockSpec((MLA_ROPE, tm), pos_t), pl.BlockSpec((MLA_ROPE, tm), pos_t)]
        + [_const_spec(w.shape) for w in weights],
        out_specs=[pl.BlockSpec((1, 1, 2 * wide, tm), tile_t), pl.BlockSpec((tm, wide), row),
                   pl.BlockSpec((tm, LANES), row), pl.BlockSpec((1, 1, wide, tm), tile_t)],
        out_shape=[jax.ShapeDtypeStruct((b, nt, 2 * wide, tm), BF16), jax.ShapeDtypeStruct((m, wide), BF16),
                   jax.ShapeDtypeStruct((m, LANES), BF16), jax.ShapeDtypeStruct((b, nt, wide, tm), BF16)],
        compiler_params=_params(("parallel",)),
        name="mla_proj",
    )(h, *tables, *weights)


def _mla_attn_kernel(qt_ref, kn_ref, kr_ref, vt_ref, o_ref, m_ref, l_ref, acc_ref, st_ref):
    qi = pl.program_id(2)
    tq, tk = MLA_TQ, MLA_TK
    m_ref[...] = jnp.full_like(m_ref, NEG_INF)
    l_ref[...] = jnp.zeros_like(l_ref)
    acc_ref[...] = jnp.zeros_like(acc_ref)

    def scores(slot, j):
        rows = pl.ds(pl.multiple_of(j * tk, tk), tk)
        kcat = jnp.concatenate([kn_ref[0, rows, :], kr_ref[0, rows, :]], axis=1)
        st_ref[slot] = _dot(kcat, qt_ref[0, 0])

    def consume(slot, j, masked):
        st = st_ref[slot]
        if masked:
            kchunk = lax.broadcasted_iota(jnp.int32, (tk, tq), 0) // CHUNK
            qchunk = lax.broadcasted_iota(jnp.int32, (tk, tq), 1) // CHUNK
            st = jnp.where(kchunk <= qchunk, st, NEG_INF)
        m_prev = m_ref[...]
        m_new = jnp.maximum(m_prev, jnp.max(st, axis=0, keepdims=True))
        alpha = jnp.exp(m_prev - m_new)
        p = jnp.exp(st - m_new)
        l_ref[...] = alpha * l_ref[...] + jnp.sum(p, axis=0, keepdims=True)
        acc_ref[...] = alpha * acc_ref[...] + _dot(vt_ref[0, j], p.astype(BF16))
        m_ref[...] = m_new

    scores(0, 0)

    def pair(i, carry):
        j = 2 * i
        scores(1, j + 1)
        consume(0, j, masked=False)
        scores(0, j + 2)
        consume(1, j + 1, masked=False)
        return carry

    lax.fori_loop(0, qi // 2, pair, 0)

    @pl.when(qi % 2 == 0)
    def _():
        consume(0, qi, masked=True)

    @pl.when(qi % 2 == 1)
    def _():
        scores(1, qi)
        consume(0, qi - 1, masked=False)
        consume(1, qi, masked=True)

    o_ref[0] = (acc_ref[...] * (1.0 / l_ref[...])).T.astype(BF16)


def _mla_attn(qt, kn, kr, vt):
    b, nt, _, tq = qt.shape
    s = kn.shape[1]
    assert MLA_TQ == MLA_TK == tq
    return pl.pallas_call(
        _mla_attn_kernel,
        grid=(b, MLA_HEADS, nt),
        in_specs=[
            pl.BlockSpec((1, 1, 2 * LANES, tq), lambda bi, hi, qi: (bi, qi, hi, 0)),
            pl.BlockSpec((1, s, LANES), lambda bi, hi, qi: (bi, 0, hi)),
            pl.BlockSpec((1, s, LANES), lambda bi, hi, qi: (bi, 0, 0)),
            pl.BlockSpec((1, nt, MLA_V, tq), lambda bi, hi, qi: (bi, 0, hi, 0)),
        ],
        out_specs=pl.BlockSpec((1, tq, LANES), lambda bi, hi, qi: (bi, qi, hi)),
        out_shape=jax.ShapeDtypeStruct((b, s, MLA_HEADS * MLA_V), BF16),
        scratch_shapes=[
            pltpu.VMEM((1, tq), F32),
            pltpu.VMEM((1, tq), F32),
            pltpu.VMEM((MLA_V, tq), F32),
            pltpu.VMEM((2, MLA_TK, tq), F32),
        ],
        compiler_params=_params(("parallel", "parallel", "arbitrary")),
        name="mla_attn",
    )(qt, kn, kr, vt)


def _swap_halves(w):
    half = w.shape[-1] // 2
    return jnp.concatenate([w[..., half:], w[..., :half]], axis=-1)


def _mla_layer(h, b, s, w_in, q_norm_g, kv_norm_g, w_uq, w_ukv):
    pad = LANES - MLA_ROPE
    w_cq = w_in[:, :MLA_Q_RANK]
    w_ckv = w_in[:, MLA_Q_RANK:MLA_Q_RANK + MLA_KV_RANK]
    w_kr = w_in[:, MLA_Q_RANK + MLA_KV_RANK:]
    pad_cols = lambda w: jnp.pad(w, [(0, 0)] * (w.ndim - 1) + [(0, pad)])
    w_uq = w_uq.reshape(MLA_Q_RANK, MLA_HEADS, MLA_QK)
    w_qt = jnp.pad(w_uq, [(0, 0), (0, 0), (0, 2 * LANES - MLA_QK)]).reshape(MLA_Q_RANK, -1).T
    w_qst = _swap_halves(w_uq[:, :, MLA_NOPE:]).reshape(MLA_Q_RANK, -1).T
    w_ukv = w_ukv.reshape(MLA_KV_RANK, MLA_HEADS, MLA_NOPE + MLA_V)
    w_kn = w_ukv[:, :, :MLA_NOPE].reshape(MLA_KV_RANK, MLA_HEADS * MLA_NOPE)
    w_vt = w_ukv[:, :, MLA_NOPE:].reshape(MLA_KV_RANK, MLA_HEADS * MLA_V).T
    inv = ROPE_BASE ** (-jnp.arange(0, MLA_ROPE, 2, dtype=F32) / MLA_ROPE)
    ang = jnp.arange(s, dtype=F32)[:, None] * inv[None, :]
    cos, sin = jnp.cos(ang), jnp.sin(ang)
    cos2 = jnp.concatenate([cos, cos], axis=1)
    sin2 = jnp.concatenate([-sin, sin], axis=1)
    tables = [jnp.concatenate([cos2, cos2], axis=1), jnp.concatenate([sin2, sin2], axis=1), cos2.T, sin2.T]
    bf = lambda w: w.astype(BF16)
    weights = [bf(w_cq), bf(w_ckv), bf(pad_cols(w_kr)), bf(pad_cols(_swap_halves(w_kr))),
               q_norm_g.reshape(1, -1), kv_norm_g.reshape(1, -1),
               bf(w_qt), bf(w_qst), bf(w_kn), bf(w_vt)]
    qt, kn, kr, vt = _mla_proj(h, b, s, tables, weights)
    sh = lambda t: t.reshape(b, s, t.shape[-1])
    o = _mla_attn(qt, sh(kn), sh(kr), vt)
    return o.reshape(b * s, MLA_HEADS * MLA_V)


def kernel(x, gla_w_in, gla_w_gate_up, gla_b_gate, gla_norm_g, gla_w_out,
           rpa_w_in, rpa_rel_bias, rpa_w_out,
           mla_w_in, mla_q_norm_g, mla_kv_norm_g, mla_w_uq, mla_w_ukv, mla_w_out,
           ffn_w_in, ffn_w_out, ln_g, ln_b):
    b, s, d = x.shape
    h = x.reshape(b * s, d)
    hidden = ffn_w_out.shape[1]
    n_ch = hidden // FFN_CHUNK
    for i in range(DEPTH):
        m, j = i % N_MIXERS, i // N_MIXERS
        if m == 0:
            o = _gla_layer(h, b, s, gla_w_in[j], gla_w_gate_up[j], gla_b_gate[j], gla_norm_g[j])
            w_o = gla_w_out[j]
        elif m == 1:
            o = _rpa_layer(h, b, s, rpa_w_in[j], rpa_rel_bias[j])
            w_o = rpa_w_out[j]
        else:
            o = _mla_layer(h, b, s, mla_w_in[j], mla_q_norm_g[j], mla_kv_norm_g[j], mla_w_uq[j], mla_w_ukv[j])
            w_o = mla_w_out[j]
        w_gu = ffn_w_in[i].astype(BF16).reshape(d, 2, n_ch, FFN_CHUNK).transpose(1, 2, 0, 3)
        w_dn = ffn_w_out[i].astype(BF16).reshape(n_ch, FFN_CHUNK, d)
        ln = jnp.stack([ln_g[i, 0], ln_b[i, 0], ln_g[i, 1], ln_b[i, 1]])
        h = _post(o, h, w_o.astype(BF16), ln, w_gu, w_dn)
    return h.reshape(b, s, d)
```

```python
import functools

import jax
import jax.numpy as jnp
import numpy as np
from jax import lax
from jax.experimental import pallas as pl
from jax.experimental.pallas import tpu as pltpu

F32 = jnp.float32
BF16 = jnp.bfloat16

DEPTH = 4
N_MIXERS = 3
CHUNK = 64
DEEPNORM_ALPHA = (2 * DEPTH) ** 0.25
LN_EPS = 1e-5
RMS_EPS = 1e-6
NEG_INF = -1e30

GLA_HEADS = 4
GLA_DK = 128
GLA_DV = 256
GLA_GATE_RANK = 16
GLA_TAU = 16.0
GLA_HK = GLA_HEADS * GLA_DK
GLA_HV = GLA_HEADS * GLA_DV

RPA_HEADS = 16
RPA_HEAD_DIM = 64
RPA_LEFT_CHUNKS = 8
RPA_MAX_REL = 128
RPA_GROUP = 2
RPA_GQ = RPA_GROUP * CHUNK
RPA_GK = (RPA_LEFT_CHUNKS + RPA_GROUP) * CHUNK
RPA_TQ = 512
RPA_PAIRS = RPA_HEADS // 2

MLA_HEADS = 8
MLA_NOPE = 128
MLA_ROPE = 64
MLA_V = 128
MLA_Q_RANK = 384
MLA_KV_RANK = 256
MLA_QK = MLA_NOPE + MLA_ROPE
ROPE_BASE = 10000.0
MLA_TQ = 1024
MLA_TK = 512
BF16_SUBLANES = 16
MLA_VROWS = MLA_V + BF16_SUBLANES
LOG2E = 1.4426950408889634
RPA_VROWS = 128 + BF16_SUBLANES

LANES = 128
VMEM_LIMIT = 56 * 1024 * 1024

TOKEN_TILE = 512
FFN_CHUNK = 256


def _params(sem):
    return pltpu.CompilerParams(dimension_semantics=sem, vmem_limit_bytes=VMEM_LIMIT)


def _const_spec(shape):
    nd = len(shape)
    return pl.BlockSpec(shape, lambda *_: (0,) * nd, pipeline_mode=pl.Buffered(1))


def _layer_norm(x, g, b):
    mu = jnp.mean(x, axis=-1, keepdims=True)
    xc = x - mu
    var = jnp.mean(xc * xc, axis=-1, keepdims=True)
    return xc * lax.rsqrt(var + LN_EPS) * g + b


def _dot(a, b):
    return jnp.dot(a, b, preferred_element_type=F32)


def _dot_nt(a, b):
    return lax.dot_general(a, b, (((1,), (1,)), ((), ())), preferred_element_type=F32)


def _dot_tn(a, b):
    return lax.dot_general(a, b, (((0,), (0,)), ((), ())), preferred_element_type=F32)


def _post_kernel(o_ref, h_ref, wo_ref, ln_ref, wgu_ref, wdn_ref, out_ref, xb_ref, acc_ref):
    y = _dot(o_ref[...], wo_ref[...])
    h1 = _layer_norm(DEEPNORM_ALPHA * h_ref[...] + y, ln_ref[0:1, :], ln_ref[1:2, :])
    out_ref[...] = h1
    xb_ref[...] = h1.astype(BF16)
    acc_ref[...] = jnp.zeros_like(acc_ref)

    def body(c, carry):
        xb = xb_ref[...]
        g = _dot(xb, wgu_ref[0, c])
        u = _dot(xb, wgu_ref[1, c])
        a = (g * jax.nn.sigmoid(g) * u).astype(BF16)
        acc_ref[...] += _dot(a, wdn_ref[c])
        return carry

    lax.fori_loop(0, wdn_ref.shape[0], body, 0, unroll=True)
    out_ref[...] = _layer_norm(DEEPNORM_ALPHA * out_ref[...] + acc_ref[...],
                               ln_ref[2:3, :], ln_ref[3:4, :])


def _post(o, h, w_o, ln, w_gu, w_dn):
    m, d = h.shape
    tm = TOKEN_TILE
    row = lambda i: (i, 0)
    return pl.pallas_call(
        _post_kernel,
        grid=(m // tm,),
        in_specs=[
            pl.BlockSpec((tm, o.shape[1]), row),
            pl.BlockSpec((tm, d), row),
            _const_spec(w_o.shape),
            _const_spec(ln.shape),
            _const_spec(w_gu.shape),
            _const_spec(w_dn.shape),
        ],
        out_specs=pl.BlockSpec((tm, d), row),
        out_shape=jax.ShapeDtypeStruct((m, d), F32),
        scratch_shapes=[pltpu.VMEM((tm, d), BF16), pltpu.VMEM((tm, d), F32)],
        compiler_params=_params(("parallel",)),
        name="post_ffn",
    )(o, h, w_o, ln, w_gu, w_dn)


def _gla_proj_kernel(h_ref, wm_ref, wgl_ref, wgu_ref, bg_ref, qk_ref, v_ref, r_ref, la_ref):
    xb = h_ref[...].astype(BF16)
    qk_ref[...] = _dot(xb, wm_ref[:, 0:2 * GLA_HK])
    v_ref[...] = _dot(xb, wm_ref[:, 2 * GLA_HK:2 * GLA_HK + GLA_HV]).astype(BF16)
    r_ref[...] = _dot(xb, wm_ref[:, 2 * GLA_HK + GLA_HV:])
    g_low = _dot(xb, wgl_ref[...])
    z = _dot(g_low.astype(BF16), wgu_ref[...]) + bg_ref[...]
    log_sig = jnp.minimum(z, 0.0) - jnp.log1p(jnp.exp(-jnp.abs(z)))
    la_ref[...] = log_sig / GLA_TAU


def _gla_proj(h, w_main, w_gl, w_gu, b_gate):
    m, d = h.shape
    tm = TOKEN_TILE
    row = lambda i: (i, 0)
    return pl.pallas_call(
        _gla_proj_kernel,
        grid=(m // tm,),
        in_specs=[
            pl.BlockSpec((tm, d), row),
            _const_spec(w_main.shape),
            _const_spec(w_gl.shape),
            _const_spec(w_gu.shape),
            _const_spec(b_gate.shape),
        ],
        out_specs=[
            pl.BlockSpec((tm, 2 * GLA_HK), row),
            pl.BlockSpec((tm, GLA_HV), row),
            pl.BlockSpec((tm, GLA_HV), row),
            pl.BlockSpec((tm, GLA_HK), row),
        ],
        out_shape=[
            jax.ShapeDtypeStruct((m, 2 * GLA_HK), F32),
            jax.ShapeDtypeStruct((m, GLA_HV), BF16),
            jax.ShapeDtypeStruct((m, GLA_HV), F32),
            jax.ShapeDtypeStruct((m, GLA_HK), F32),
        ],
        compiler_params=_params(("parallel",)),
        name="gla_proj",
    )(h, w_main, w_gl, w_gu, b_gate)


def _split3_bf16(x):
    hi = x.astype(BF16)
    r1 = x - hi.astype(F32)
    mid = r1.astype(BF16)
    lo = (r1 - mid.astype(F32)).astype(BF16)
    return hi, mid, lo


def _gla_core_kernel(qk_ref, v_ref, r_ref, la_ref, g_ref, y_ref, state_ref, *, n_chunks):
    @pl.when(pl.program_id(1) == 0)
    def _():
        state_ref[...] = jnp.zeros_like(state_ref)

    row_i = lax.broadcasted_iota(jnp.int32, (CHUNK, CHUNK), 0)
    col_i = lax.broadcasted_iota(jnp.int32, (CHUNK, CHUNK), 1)
    causal = col_i <= row_i
    tri = causal.astype(BF16)

    for ci in range(n_chunks):
        rows = slice(ci * CHUNK, (ci + 1) * CHUNK)
        la = la_ref[0, rows, :]
        hi, mid, lo = _split3_bf16(la)
        cum = _dot(tri, hi) + _dot(tri, mid) + _dot(tri, lo)
        tot = cum[CHUNK - 1:CHUNK, :]
        q = qk_ref[0, rows, 0:GLA_HK] * (GLA_DK ** -0.5)
        k = qk_ref[0, rows, GLA_HK:2 * GLA_HK]
        q_dec = (q * jnp.exp(cum)).astype(BF16)
        k_inv = (k * jnp.exp(-cum)).astype(BF16)
        k_end = (k * jnp.exp(tot - cum)).astype(BF16)
        decay = jnp.exp(tot)
        for h in range(GLA_HEADS):
            kc = slice(h * GLA_DK, (h + 1) * GLA_DK)
            vc = slice(h * GLA_DV, (h + 1) * GLA_DV)
            qd, ki, ke = q_dec[:, kc], k_inv[:, kc], k_end[:, kc]
            vh = v_ref[0, rows, vc]
            st = state_ref[h]
            a = jnp.where(causal, _dot_nt(qd, ki), 0.0)
            o = _dot(a.astype(BF16), vh) + _dot_nt(qd, st.astype(BF16))
            state_ref[h] = st * decay[:, kc] + _dot_tn(vh, ke)
            mu = jnp.mean(o, axis=-1, keepdims=True)
            oc = o - mu
            var = jnp.mean(oc * oc, axis=-1, keepdims=True)
            on = oc * lax.rsqrt(var + LN_EPS) * g_ref[:, vc]
            r = r_ref[0, rows, vc]
            y_ref[0, rows, vc] = (r * jax.nn.sigmoid(r) * on).astype(BF16)


def _gla_core(qk, v, r, la, norm_g, *, block):
    b, s, _ = qk.shape
    blk = lambda bi, ti: (bi, ti, 0)
    return pl.pallas_call(
        functools.partial(_gla_core_kernel, n_chunks=block // CHUNK),
        grid=(b, s // block),
        in_specs=[
            pl.BlockSpec((1, block, 2 * GLA_HK), blk),
            pl.BlockSpec((1, block, GLA_HV), blk),
            pl.BlockSpec((1, block, GLA_HV), blk),
            pl.BlockSpec((1, block, GLA_HK), blk),
            pl.BlockSpec((1, GLA_HV), lambda bi, ti: (0, 0)),
        ],
        out_specs=pl.BlockSpec((1, block, GLA_HV), blk),
        out_shape=jax.ShapeDtypeStruct((b, s, GLA_HV), BF16),
        scratch_shapes=[pltpu.VMEM((GLA_HEADS, GLA_DV, GLA_DK), F32)],
        compiler_params=_params(("parallel", "arbitrary")),
        name="gla_core",
    )(qk, v, r, la, norm_g)


def _gla_layer(h, b, s, w_in, w_gate_up, b_gate, norm_g):
    d = h.shape[1]
    n_main = 2 * GLA_HK + 2 * GLA_HV
    w_main = w_in[:, :n_main].astype(BF16)
    w_gl = jnp.zeros((d, LANES), BF16).at[:, :GLA_GATE_RANK].set(w_in[:, n_main:].astype(BF16))
    w_gu = jnp.zeros((LANES, GLA_HK), BF16).at[:GLA_GATE_RANK, :].set(w_gate_up.astype(BF16))
    qk, v, r, la = _gla_proj(h, w_main, w_gl, w_gu, b_gate.reshape(1, GLA_HK))
    sh = lambda t: t.reshape(b, s, t.shape[-1])
    y = _gla_core(sh(qk), sh(v), sh(r), sh(la), norm_g.reshape(1, GLA_HV), block=256)
    return y.reshape(b * s, GLA_HV)


def _rpa_proj_kernel(h_ref, wqt_ref, wk_ref, wvt_ref, qt_ref, k_ref, vt_ref):
    xb = h_ref[...].astype(BF16)
    tm = h_ref.shape[0]
    qt_ref[0] = (_dot_nt(wqt_ref[...], xb) * (RPA_HEAD_DIM ** -0.5 * LOG2E)).astype(BF16)
    k_ref[...] = _dot(xb, wk_ref[...]).astype(BF16)
    vt = _dot_nt(wvt_ref[...], xb).astype(BF16)
    for p in range(RPA_PAIRS):
        r0 = p * RPA_VROWS
        vt_ref[0, r0:r0 + LANES, :] = vt[p * LANES:(p + 1) * LANES]
        vt_ref[0, r0 + LANES:r0 + RPA_VROWS, :] = jnp.ones((BF16_SUBLANES, tm), BF16)


def _rpa_proj(h, b, s, w_qt, w_k, w_vt):
    m, d = h.shape
    tm = RPA_TQ
    nt = s // tm
    row = lambda i: (i, 0)
    col = lambda i: (i // nt, 0, i % nt)
    return pl.pallas_call(
        _rpa_proj_kernel,
        grid=(m // tm,),
        in_specs=[pl.BlockSpec((tm, d), row), _const_spec(w_qt.shape), _const_spec(w_k.shape),
                  _const_spec(w_vt.shape)],
        out_specs=[pl.BlockSpec((1, d, tm), col), pl.BlockSpec((tm, d), row),
                   pl.BlockSpec((1, RPA_PAIRS * RPA_VROWS, tm), col)],
        out_shape=[jax.ShapeDtypeStruct((b, d, s), BF16), jax.ShapeDtypeStruct((m, d), BF16),
                   jax.ShapeDtypeStruct((b, RPA_PAIRS * RPA_VROWS, s), BF16)],
        compiler_params=_params(("parallel",)),
        name="rpa_proj",
    )(h, w_qt, w_k, w_vt)


def _rpa_core_kernel(qt_ref, kp_ref, kc_ref, vtp_ref, vtc_ref, bias_ref, o_ref, kw_ref, vw_ref, st_ref, mx_ref):
    t = pl.program_id(1)
    tq = o_ref.shape[1]
    kw_ref[0:tq, :] = kp_ref[0]
    kw_ref[tq:2 * tq, :] = kc_ref[0]
    vw_ref[:, 0:tq] = vtp_ref[0]
    vw_ref[:, tq:2 * tq] = vtc_ref[0]

    first_head = lax.broadcasted_iota(jnp.int32, (LANES, RPA_GQ), 0) < RPA_HEAD_DIM
    key_row = lax.broadcasted_iota(jnp.int32, (RPA_GK, 2 * RPA_GQ), 0)

    def scores(slot, g, p):
        q0 = g * RPA_GQ
        feat = slice(p * LANES, (p + 1) * LANES)
        qt = qt_ref[0, feat, q0:q0 + RPA_GQ]
        zero = jnp.zeros_like(qt)
        q_blk = jnp.concatenate([jnp.where(first_head, qt, zero), jnp.where(first_head, zero, qt)], axis=1)
        st = _dot(kw_ref[q0:q0 + RPA_GK, feat], q_blk) + bias_ref[p]
        st = jnp.where(jnp.logical_and(t == 0, key_row + q0 < tq), NEG_INF, st)
        st_ref[slot] = st
        mx_ref[slot] = jnp.max(st, axis=0, keepdims=True)

    def consume(slot, g, p):
        q0 = g * RPA_GQ
        feat = slice(p * LANES, (p + 1) * LANES)
        e = jnp.exp2(st_ref[slot] - mx_ref[slot]).astype(BF16)
        ot = _dot(vw_ref[p * RPA_VROWS:(p + 1) * RPA_VROWS, q0:q0 + RPA_GK], e)
        on = ot[0:LANES, :] * (1.0 / ot[LANES:LANES + 1, :])
        pair_t = jnp.where(first_head, on[:, 0:RPA_GQ], on[:, RPA_GQ:2 * RPA_GQ])
        o_ref[0, q0:q0 + RPA_GQ, feat] = pair_t.T.astype(BF16)

    units = [(g, p) for g in range(tq // RPA_GQ) for p in range(RPA_PAIRS)]
    scores(0, *units[0])
    for i, unit in enumerate(units):
        if i + 1 < len(units):
            scores((i + 1) % 2, *units[i + 1])
        consume(i % 2, *unit)


def _rpa_core(qt, k, vt, bias_t):
    b, s, d = k.shape
    tq = RPA_TQ
    vrows = vt.shape[1]
    prev = lambda ti: jnp.maximum(ti - 1, 0)
    return pl.pallas_call(
        _rpa_core_kernel,
        grid=(b, s // tq),
        in_specs=[
            pl.BlockSpec((1, d, tq), lambda bi, ti: (bi, 0, ti)),
            pl.BlockSpec((1, tq, d), lambda bi, ti: (bi, prev(ti), 0)),
            pl.BlockSpec((1, tq, d), lambda bi, ti: (bi, ti, 0)),
            pl.BlockSpec((1, vrows, tq), lambda bi, ti: (bi, 0, prev(ti))),
            pl.BlockSpec((1, vrows, tq), lambda bi, ti: (bi, 0, ti)),
            _const_spec(bias_t.shape),
        ],
        out_specs=pl.BlockSpec((1, tq, d), lambda bi, ti: (bi, ti, 0)),
        out_shape=jax.ShapeDtypeStruct((b, s, d), BF16),
        scratch_shapes=[pltpu.VMEM((2 * tq, d), BF16), pltpu.VMEM((vrows, 2 * tq), BF16),
                        pltpu.VMEM((2, RPA_GK, 2 * RPA_GQ), F32), pltpu.VMEM((2, 1, 2 * RPA_GQ), F32)],
        compiler_params=_params(("parallel", "parallel")),
        name="rpa_core",
    )(qt, k, k, vt, vt, bias_t)


def _rpa_group_bias_t(rel_bias):
    gq, gk = RPA_GQ, RPA_GK
    span = gq + gk - 1
    diag = np.arange(span)
    rel = np.clip(RPA_LEFT_CHUNKS * CHUNK + gq - 1 - diag, -RPA_MAX_REL, RPA_MAX_REL) + RPA_MAX_REL
    u = jnp.pad(rel_bias[:, rel].astype(F32), ((0, 0), (0, 1)))
    shifted = jnp.tile(u, (1, gq))[:, :gq * span].reshape(-1, gq, span)
    table = shifted[:, :, gq - 1:gq - 1 + gk]
    r = np.arange(gq)[:, None]
    w = np.arange(gk)[None, :]
    band = w - (r // CHUNK) * CHUNK
    in_band = (band >= 0) & (band < (RPA_LEFT_CHUNKS + 1) * CHUNK)
    table = jnp.where(in_band[None], table * LOG2E, NEG_INF)
    table_t = table.transpose(0, 2, 1).reshape(RPA_PAIRS, 2, gk, gq)
    return table_t.transpose(0, 2, 1, 3).reshape(RPA_PAIRS, gk, 2 * gq)


def _rpa_layer(h, b, s, w_in, rel_bias):
    d = h.shape[1]
    w = w_in.astype(BF16)
    qt, k, vt = _rpa_proj(h, b, s, w[:, 0:d].T, w[:, d:2 * d], w[:, 2 * d:3 * d].T)
    o = _rpa_core(qt, k.reshape(b, s, d), vt, _rpa_group_bias_t(rel_bias))
    return o.reshape(b * s, d)


def _rms_norm(x, g):
    return x * lax.rsqrt(jnp.mean(x * x, axis=-1, keepdims=True) + RMS_EPS) * g


def _mla_proj_kernel(h_ref, cos_ref, sin_ref, cost_ref, sint_ref, wcq_ref, wckv_ref, wkr_ref, wkrs_ref,
                     gq_ref, gkv_ref, wqt_ref, wqst_ref, wkn_ref, wvt_ref,
                     qt_ref, kn_ref, kr_ref, vt_ref):
    xb = h_ref[...].astype(BF16)
    tm = h_ref.shape[0]
    scale = MLA_QK ** -0.5 * LOG2E
    cq = _rms_norm(_dot(xb, wcq_ref[...]), gq_ref[...]).astype(BF16)
    ckv = _rms_norm(_dot(xb, wckv_ref[...]), gkv_ref[...]).astype(BF16)
    kn_ref[...] = _dot(ckv, wkn_ref[...]).astype(BF16)
    kr = _dot(xb, wkr_ref[...]) * cos_ref[...] + _dot(xb, wkrs_ref[...]) * sin_ref[...]
    kr_ref[...] = kr.astype(BF16)
    vt = _dot_nt(wvt_ref[...], ckv).astype(BF16)
    for h in range(MLA_HEADS):
        r0 = h * MLA_VROWS
        vt_ref[0, 0, r0:r0 + MLA_V, :] = vt[h * MLA_V:(h + 1) * MLA_V]
        vt_ref[0, 0, r0 + MLA_V:r0 + MLA_VROWS, :] = jnp.ones((BF16_SUBLANES, tm), BF16)
    a = _dot_nt(wqt_ref[...], cq)
    bsw = _dot_nt(wqst_ref[...], cq)
    cos_t = cost_ref[...]
    sin_t = sint_ref[...]
    hq = 2 * LANES
    for h in range(MLA_HEADS):
        r0 = h * hq
        qt_ref[0, 0, r0:r0 + MLA_NOPE, :] = (a[r0:r0 + MLA_NOPE] * scale).astype(BF16)
        rope = a[r0 + MLA_NOPE:r0 + MLA_QK] * cos_t + bsw[h * MLA_ROPE:(h + 1) * MLA_ROPE] * sin_t
        qt_ref[0, 0, r0 + MLA_NOPE:r0 + MLA_QK, :] = (rope * scale).astype(BF16)
        qt_ref[0, 0, r0 + MLA_QK:r0 + hq, :] = jnp.zeros((hq - MLA_QK, qt_ref.shape[3]), BF16)


def _mla_proj(h, b, s, tables, weights):
    m, d = h.shape
    tm = MLA_TK
    nt = s // tm
    row = lambda i: (i, 0)
    pos = lambda i: (i % nt, 0)
    pos_t = lambda i: (0, i % nt)
    tile_t = lambda i: (i // nt, i % nt, 0, 0)
    wide = MLA_HEADS * LANES
    return pl.pallas_call(
        _mla_proj_kernel,
        grid=(m // tm,),
        in_specs=[pl.BlockSpec((tm, d), row), pl.BlockSpec((tm, LANES), pos), pl.BlockSpec((tm, LANES), pos),
                  pl.BlockSpec((MLA_ROPE, tm), pos_t), pl.BlockSpec((MLA_ROPE, tm), pos_t)]
        + [_const_spec(w.shape) for w in weights],
        out_specs=[pl.BlockSpec((1, 1, 2 * wide, tm), tile_t), pl.BlockSpec((tm, wide), row),
                   pl.BlockSpec((tm, LANES), row), pl.BlockSpec((1, 1, MLA_HEADS * MLA_VROWS, tm), tile_t)],
        out_shape=[jax.ShapeDtypeStruct((b, nt, 2 * wide, tm), BF16), jax.ShapeDtypeStruct((m, wide), BF16),
                   jax.ShapeDtypeStruct((m, LANES), BF16),
                   jax.ShapeDtypeStruct((b, nt, MLA_HEADS * MLA_VROWS, tm), BF16)],
        compiler_params=_params(("parallel",)),
        name="mla_proj",
    )(h, *tables, *weights)


def _mla_attn_kernel(qt_ref, kn_ref, kr_ref, vt_ref, o_ref, m_ref, acc_ref, st_ref, mx_ref):
    qi = pl.program_id(2)
    tq, tk = MLA_TQ, MLA_TK
    assert tq == 2 * tk
    m_ref[...] = jnp.full_like(m_ref, NEG_INF)
    acc_ref[...] = jnp.zeros_like(acc_ref)

    def scores(slot, j):
        rows = pl.ds(pl.multiple_of(j * tk, tk), tk)
        kcat = jnp.concatenate([kn_ref[0, rows, :], kr_ref[0, rows, :]], axis=1)
        st_ref[slot, :, 0:tk] = _dot(kcat, qt_ref[0, 0])
        st_ref[slot, :, tk:tq] = _dot(kcat, qt_ref[0, 1])
        mx_ref[slot] = jnp.max(st_ref[slot], axis=0, keepdims=True)

    def consume(slot, j, first_key_chunk=None):
        st = st_ref[slot]
        if first_key_chunk is None:
            mx = mx_ref[slot]
        else:
            kchunk = first_key_chunk + lax.broadcasted_iota(jnp.int32, (tk, tq), 0) // CHUNK
            qchunk = lax.broadcasted_iota(jnp.int32, (tk, tq), 1) // CHUNK
            st = jnp.where(kchunk <= qchunk, st, NEG_INF)
            mx = jnp.max(st, axis=0, keepdims=True)
        m_prev = m_ref[...]
        m_new = jnp.maximum(m_prev, mx)
        alpha = jnp.exp2(m_prev - m_new)
        p = jnp.exp2(st - m_new).astype(BF16)
        acc_ref[...] = alpha * acc_ref[...] + _dot(vt_ref[0, j], p)
        m_ref[...] = m_new

    scores(0, 0)

    def pair(i, carry):
        j = 2 * i
        scores(1, j + 1)
        consume(0, j)
        scores(0, j + 2)
        consume(1, j + 1)
        return carry

    lax.fori_loop(0, qi, pair, 0)
    scores(1, 2 * qi + 1)
    consume(0, 2 * qi, first_key_chunk=0)
    consume(1, 2 * qi + 1, first_key_chunk=tk // CHUNK)

    acc = acc_ref[...]
    o_ref[0] = (acc[0:MLA_V] * (1.0 / acc[MLA_V:MLA_V + 1])).T.astype(BF16)


def _mla_attn(qt, kn, kr, vt):
    b, nt, _, tk = qt.shape
    s = kn.shape[1]
    tq = MLA_TQ
    assert MLA_TK == tk
    return pl.pallas_call(
        _mla_attn_kernel,
        grid=(b, MLA_HEADS, s // tq),
        in_specs=[
            pl.BlockSpec((1, tq // tk, 2 * LANES, tk), lambda bi, hi, qi: (bi, qi, hi, 0)),
            pl.BlockSpec((1, s, LANES), lambda bi, hi, qi: (bi, 0, hi)),
            pl.BlockSpec((1, s, LANES), lambda bi, hi, qi: (bi, 0, 0)),
            pl.BlockSpec((1, nt, MLA_VROWS, tk), lambda bi, hi, qi: (bi, 0, hi, 0)),
        ],
        out_specs=pl.BlockSpec((1, tq, LANES), lambda bi, hi, qi: (bi, qi, hi)),
        out_shape=jax.ShapeDtypeStruct((b, s, MLA_HEADS * MLA_V), BF16),
        scratch_shapes=[
            pltpu.VMEM((1, tq), F32),
            pltpu.VMEM((MLA_VROWS, tq), F32),
            pltpu.VMEM((2, tk, tq), F32),
            pltpu.VMEM((2, 1, tq), F32),
        ],
        compiler_params=_params(("parallel", "parallel", "arbitrary")),
        name="mla_attn",
    )(qt, kn, kr, vt)


def _swap_halves(w):
    half = w.shape[-1] // 2
    return jnp.concatenate([w[..., half:], w[..., :half]], axis=-1)


def _mla_layer(h, b, s, w_in, q_norm_g, kv_norm_g, w_uq, w_ukv):
    pad = LANES - MLA_ROPE
    w_cq = w_in[:, :MLA_Q_RANK]
    w_ckv = w_in[:, MLA_Q_RANK:MLA_Q_RANK + MLA_KV_RANK]
    w_kr = w_in[:, MLA_Q_RANK + MLA_KV_RANK:]
    pad_cols = lambda w: jnp.pad(w, [(0, 0)] * (w.ndim - 1) + [(0, pad)])
    w_uq = w_uq.reshape(MLA_Q_RANK, MLA_HEADS, MLA_QK)
    w_qt = jnp.pad(w_uq, [(0, 0), (0, 0), (0, 2 * LANES - MLA_QK)]).reshape(MLA_Q_RANK, -1).T
    w_qst = _swap_halves(w_uq[:, :, MLA_NOPE:]).reshape(MLA_Q_RANK, -1).T
    w_ukv = w_ukv.reshape(MLA_KV_RANK, MLA_HEADS, MLA_NOPE + MLA_V)
    w_kn = w_ukv[:, :, :MLA_NOPE].reshape(MLA_KV_RANK, MLA_HEADS * MLA_NOPE)
    w_vt = w_ukv[:, :, MLA_NOPE:].reshape(MLA_KV_RANK, MLA_HEADS * MLA_V).T
    inv = ROPE_BASE ** (-jnp.arange(0, MLA_ROPE, 2, dtype=F32) / MLA_ROPE)
    ang = jnp.arange(s, dtype=F32)[:, None] * inv[None, :]
    cos, sin = jnp.cos(ang), jnp.sin(ang)
    cos2 = jnp.concatenate([cos, cos], axis=1)
    sin2 = jnp.concatenate([-sin, sin], axis=1)
    tables = [jnp.concatenate([cos2, cos2], axis=1), jnp.concatenate([sin2, sin2], axis=1), cos2.T, sin2.T]
    bf = lambda w: w.astype(BF16)
    weights = [bf(w_cq), bf(w_ckv), bf(pad_cols(w_kr)), bf(pad_cols(_swap_halves(w_kr))),
               q_norm_g.reshape(1, -1), kv_norm_g.reshape(1, -1),
               bf(w_qt), bf(w_qst), bf(w_kn), bf(w_vt)]
    qt, kn, kr, vt = _mla_proj(h, b, s, tables, weights)
    sh = lambda t: t.reshape(b, s, t.shape[-1])
    o = _mla_attn(qt, sh(kn), sh(kr), vt)
    return o.reshape(b * s, MLA_HEADS * MLA_V)


def kernel(x, gla_w_in, gla_w_gate_up, gla_b_gate, gla_norm_g, gla_w_out,
           rpa_w_in, rpa_rel_bias, rpa_w_out,
           mla_w_in, mla_q_norm_g, mla_kv_norm_g, mla_w_uq, mla_w_ukv, mla_w_out,
           ffn_w_in, ffn_w_out, ln_g, ln_b):
    b, s, d = x.shape
    h = x.reshape(b * s, d)
    hidden = ffn_w_out.shape[1]
    n_ch = hidden // FFN_CHUNK
    for i in range(DEPTH):
        m, j = i % N_MIXERS, i // N_MIXERS
        if m == 0:
            o = _gla_layer(h, b, s, gla_w_in[j], gla_w_gate_up[j], gla_b_gate[j], gla_norm_g[j])
            w_o = gla_w_out[j]
        elif m == 1:
            o = _rpa_layer(h, b, s, rpa_w_in[j], rpa_rel_bias[j])
            w_o = rpa_w_out[j]
        else:
            o = _mla_layer(h, b, s, mla_w_in[j], mla_q_norm_g[j], mla_kv_norm_g[j], mla_w_uq[j], mla_w_ukv[j])
            w_o = mla_w_out[j]
        w_gu = ffn_w_in[i].astype(BF16).reshape(d, 2, n_ch, FFN_CHUNK).transpose(1, 2, 0, 3)
        w_dn = ffn_w_out[i].astype(BF16).reshape(n_ch, FFN_CHUNK, d)
        ln = jnp.stack([ln_g[i, 0], ln_b[i, 0], ln_g[i, 1], ln_b[i, 1]])
        h = _post(o, h, w_o.astype(BF16), ln, w_gu, w_dn)
    return h.reshape(b, s, d)
```

```python
import functools

import jax
import jax.numpy as jnp
import numpy as np
from jax import lax
from jax.experimental import pallas as pl
from jax.experimental.pallas import tpu as pltpu

F32 = jnp.float32
BF16 = jnp.bfloat16

DEPTH = 4
N_MIXERS = 3
CHUNK = 64
DEEPNORM_ALPHA = (2 * DEPTH) ** 0.25
LN_EPS = 1e-5
RMS_EPS = 1e-6
NEG_INF = -1e30

GLA_HEADS = 4
GLA_DK = 128
GLA_DV = 256
GLA_GATE_RANK = 16
GLA_TAU = 16.0
GLA_HK = GLA_HEADS * GLA_DK
GLA_HV = GLA_HEADS * GLA_DV
GLA_BLOCK = 512

RPA_HEADS = 16
RPA_HEAD_DIM = 64
RPA_LEFT_CHUNKS = 8
RPA_MAX_REL = 128
RPA_GROUP = 2
RPA_GQ = RPA_GROUP * CHUNK
RPA_GK = (RPA_LEFT_CHUNKS + RPA_GROUP) * CHUNK
RPA_TQ = 512
RPA_PAIRS = RPA_HEADS // 2

MLA_HEADS = 8
MLA_NOPE = 128
MLA_ROPE = 64
MLA_V = 128
MLA_Q_RANK = 384
MLA_KV_RANK = 256
MLA_QK = MLA_NOPE + MLA_ROPE
ROPE_BASE = 10000.0
MLA_TQ = 1024
MLA_TK = 512
BF16_SUBLANES = 16
MLA_VROWS = MLA_V + BF16_SUBLANES
LOG2E = 1.4426950408889634
RPA_VROWS = 128 + BF16_SUBLANES

LANES = 128
VMEM_LIMIT = 56 * 1024 * 1024

TOKEN_TILE = 512
POST_SUBTILES = 2
FFN_CHUNK = 256


def _params(sem):
    return pltpu.CompilerParams(dimension_semantics=sem, vmem_limit_bytes=VMEM_LIMIT)


def _const_spec(shape):
    nd = len(shape)
    return pl.BlockSpec(shape, lambda *_: (0,) * nd, pipeline_mode=pl.Buffered(1))


def _layer_norm(x, g, b):
    mu = jnp.mean(x, axis=-1, keepdims=True)
    xc = x - mu
    var = jnp.mean(xc * xc, axis=-1, keepdims=True)
    return xc * lax.rsqrt(var + LN_EPS) * g + b


def _dot(a, b):
    return jnp.dot(a, b, preferred_element_type=F32)


def _dot_nt(a, b):
    return lax.dot_general(a, b, (((1,), (1,)), ((), ())), preferred_element_type=F32)


def _dot_tn(a, b):
    return lax.dot_general(a, b, (((0,), (0,)), ((), ())), preferred_element_type=F32)


def _post_kernel(o_ref, h_ref, wo_ref, ln_ref, wgu_ref, wdn_ref, out_ref, xb_ref):
    tm = out_ref.shape[0]
    hidden = wdn_ref.shape[0]
    halves = [slice(r0, r0 + TOKEN_TILE) for r0 in range(0, tm, TOKEN_TILE)]

    def mixer_out_norm(rows):
        y = _dot(o_ref[rows, :], wo_ref[...])
        h1 = _layer_norm(DEEPNORM_ALPHA * h_ref[rows, :] + y, ln_ref[0:1, :], ln_ref[1:2, :])
        out_ref[rows, :] = h1
        xb_ref[rows, :] = h1.astype(BF16)

    def ffn_norm(rows):
        xb = xb_ref[rows, :]
        acc = None
        for c0 in range(0, hidden, FFN_CHUNK):
            g = _dot(xb, wgu_ref[:, c0:c0 + FFN_CHUNK])
            u = _dot(xb, wgu_ref[:, hidden + c0:hidden + c0 + FFN_CHUNK])
            a = (g * jax.nn.sigmoid(g) * u).astype(BF16)
            part = _dot(a, wdn_ref[c0:c0 + FFN_CHUNK, :])
            acc = part if acc is None else acc + part
        out_ref[rows, :] = _layer_norm(DEEPNORM_ALPHA * out_ref[rows, :] + acc, ln_ref[2:3, :], ln_ref[3:4, :])

    for rows in halves:
        mixer_out_norm(rows)
    for rows in halves:
        ffn_norm(rows)


def _post(o, h, w_o, ln, w_gu, w_dn):
    m, d = h.shape
    tm = POST_SUBTILES * TOKEN_TILE
    row = lambda i: (i, 0)
    return pl.pallas_call(
        _post_kernel,
        grid=(m // tm,),
        in_specs=[
            pl.BlockSpec((tm, o.shape[1]), row),
            pl.BlockSpec((tm, d), row),
            _const_spec(w_o.shape),
            _const_spec(ln.shape),
            _const_spec(w_gu.shape),
            _const_spec(w_dn.shape),
        ],
        out_specs=pl.BlockSpec((tm, d), row),
        out_shape=jax.ShapeDtypeStruct((m, d), F32),
        scratch_shapes=[pltpu.VMEM((tm, d), BF16)],
        compiler_params=_params(("parallel",)),
        name="post_ffn",
    )(o, h, w_o, ln, w_gu, w_dn)


def _gla_proj_kernel(h_ref, wm_ref, wgl_ref, wgu_ref, bg_ref, qk_ref, v_ref, r_ref, la_ref):
    xb = h_ref[...].astype(BF16)
    g_low = _dot(xb, wgl_ref[...])
    z = _dot(g_low.astype(BF16), wgu_ref[...]) + bg_ref[...]
    log_sig = jnp.minimum(z, 0.0) - jnp.log(1.0 + jnp.exp(-jnp.abs(z)))
    la_ref[...] = log_sig / GLA_TAU
    qk_ref[...] = _dot(xb, wm_ref[:, 0:2 * GLA_HK])
    v_ref[...] = _dot(xb, wm_ref[:, 2 * GLA_HK:2 * GLA_HK + GLA_HV]).astype(BF16)
    r_ref[...] = _dot(xb, wm_ref[:, 2 * GLA_HK + GLA_HV:])


def _gla_proj(h, w_main, w_gl, w_gu, b_gate):
    m, d = h.shape
    tm = TOKEN_TILE
    row = lambda i: (i, 0)
    return pl.pallas_call(
        _gla_proj_kernel,
        grid=(m // tm,),
        in_specs=[
            pl.BlockSpec((tm, d), row),
            _const_spec(w_main.shape),
            _const_spec(w_gl.shape),
            _const_spec(w_gu.shape),
            _const_spec(b_gate.shape),
        ],
        out_specs=[
            pl.BlockSpec((tm, 2 * GLA_HK), row),
            pl.BlockSpec((tm, GLA_HV), row),
            pl.BlockSpec((tm, GLA_HV), row),
            pl.BlockSpec((tm, GLA_HK), row),
        ],
        out_shape=[
            jax.ShapeDtypeStruct((m, 2 * GLA_HK), F32),
            jax.ShapeDtypeStruct((m, GLA_HV), BF16),
            jax.ShapeDtypeStruct((m, GLA_HV), F32),
            jax.ShapeDtypeStruct((m, GLA_HK), F32),
        ],
        compiler_params=_params(("parallel",)),
        name="gla_proj",
    )(h, w_main, w_gl, w_gu, b_gate)


def _split3_bf16(x):
    hi = x.astype(BF16)
    r1 = x - hi.astype(F32)
    mid = r1.astype(BF16)
    lo = (r1 - mid.astype(F32)).astype(BF16)
    return hi, mid, lo


def _gla_core_kernel(qk_ref, v_ref, r_ref, la_ref, g_ref, y_ref, state_ref, *, n_chunks):
    @pl.when(pl.program_id(1) == 0)
    def _():
        state_ref[...] = jnp.zeros_like(state_ref)

    row_i = lax.broadcasted_iota(jnp.int32, (CHUNK, CHUNK), 0)
    col_i = lax.broadcasted_iota(jnp.int32, (CHUNK, CHUNK), 1)
    causal = col_i <= row_i
    tri = causal.astype(BF16)

    def prepare(ci):
        rows = slice(ci * CHUNK, (ci + 1) * CHUNK)
        la = la_ref[0, rows, :]
        hi, mid, lo = _split3_bf16(la)
        cum = _dot(tri, hi) + _dot(tri, mid) + _dot(tri, lo)
        tot = cum[CHUNK - 1:CHUNK, :]
        q = qk_ref[0, rows, 0:GLA_HK] * (GLA_DK ** -0.5)
        k = qk_ref[0, rows, GLA_HK:2 * GLA_HK]
        q_dec = (q * jnp.exp(cum)).astype(BF16)
        k_inv = (k * jnp.exp(-cum)).astype(BF16)
        k_end = (k * jnp.exp(tot - cum)).astype(BF16)
        decay = jnp.exp(tot)
        o_intra = []
        for h in range(GLA_HEADS):
            kc = slice(h * GLA_DK, (h + 1) * GLA_DK)
            vc = slice(h * GLA_DV, (h + 1) * GLA_DV)
            a = jnp.where(causal, _dot_nt(q_dec[:, kc], k_inv[:, kc]), 0.0)
            o_intra.append(_dot(a.astype(BF16), v_ref[0, rows, vc]))
        return q_dec, k_end, decay, o_intra

    def recur(ci, prepared):
        rows = slice(ci * CHUNK, (ci + 1) * CHUNK)
        q_dec, k_end, decay, o_intra = prepared
        for h in range(GLA_HEADS):
            kc = slice(h * GLA_DK, (h + 1) * GLA_DK)
            vc = slice(h * GLA_DV, (h + 1) * GLA_DV)
            vh = v_ref[0, rows, vc]
            st = state_ref[h]
            o = o_intra[h] + _dot_nt(q_dec[:, kc], st.astype(BF16))
            state_ref[h] = st * decay[:, kc] + _dot_tn(vh, k_end[:, kc])
            mu = jnp.mean(o, axis=-1, keepdims=True)
            oc = o - mu
            var = jnp.mean(oc * oc, axis=-1, keepdims=True)
            on = oc * lax.rsqrt(var + LN_EPS) * g_ref[:, vc]
            r = r_ref[0, rows, vc]
            y_ref[0, rows, vc] = (r * jax.nn.sigmoid(r) * on).astype(BF16)

    prepared = prepare(0)
    for ci in range(n_chunks):
        upcoming = prepare(ci + 1) if ci + 1 < n_chunks else None
        recur(ci, prepared)
        prepared = upcoming


def _gla_core(qk, v, r, la, norm_g, *, block):
    b, s, _ = qk.shape
    blk = lambda bi, ti: (bi, ti, 0)
    return pl.pallas_call(
        functools.partial(_gla_core_kernel, n_chunks=block // CHUNK),
        grid=(b, s // block),
        in_specs=[
            pl.BlockSpec((1, block, 2 * GLA_HK), blk),
            pl.BlockSpec((1, block, GLA_HV), blk),
            pl.BlockSpec((1, block, GLA_HV), blk),
            pl.BlockSpec((1, block, GLA_HK), blk),
            pl.BlockSpec((1, GLA_HV), lambda bi, ti: (0, 0)),
        ],
        out_specs=pl.BlockSpec((1, block, GLA_HV), blk),
        out_shape=jax.ShapeDtypeStruct((b, s, GLA_HV), BF16),
        scratch_shapes=[pltpu.VMEM((GLA_HEADS, GLA_DV, GLA_DK), F32)],
        compiler_params=_params(("parallel", "arbitrary")),
        name="gla_core",
    )(qk, v, r, la, norm_g)


def _gla_layer(h, b, s, w_in, w_gate_up, b_gate, norm_g):
    d = h.shape[1]
    n_main = 2 * GLA_HK + 2 * GLA_HV
    w_main = w_in[:, :n_main].astype(BF16)
    w_gl = jnp.zeros((d, LANES), BF16).at[:, :GLA_GATE_RANK].set(w_in[:, n_main:].astype(BF16))
    w_gu = jnp.zeros((LANES, GLA_HK), BF16).at[:GLA_GATE_RANK, :].set(w_gate_up.astype(BF16))
    qk, v, r, la = _gla_proj(h, w_main, w_gl, w_gu, b_gate.reshape(1, GLA_HK))
    sh = lambda t: t.reshape(b, s, t.shape[-1])
    y = _gla_core(sh(qk), sh(v), sh(r), sh(la), norm_g.reshape(1, GLA_HV), block=GLA_BLOCK)
    return y.reshape(b * s, GLA_HV)


def _rpa_proj_kernel(h_ref, wqt_ref, wk_ref, wvt_ref, qt_ref, k_ref, vt_ref):
    xb = h_ref[...].astype(BF16)
    tm = h_ref.shape[0]
    qt_ref[0] = (_dot_nt(wqt_ref[...], xb) * (RPA_HEAD_DIM ** -0.5 * LOG2E)).astype(BF16)
    k_ref[...] = _dot(xb, wk_ref[...]).astype(BF16)
    vt = _dot_nt(wvt_ref[...], xb).astype(BF16)
    for p in range(RPA_PAIRS):
        r0 = p * RPA_VROWS
        vt_ref[0, r0:r0 + LANES, :] = vt[p * LANES:(p + 1) * LANES]
        vt_ref[0, r0 + LANES:r0 + RPA_VROWS, :] = jnp.ones((BF16_SUBLANES, tm), BF16)


def _rpa_proj(h, b, s, w_qt, w_k, w_vt):
    m, d = h.shape
    tm = RPA_TQ
    nt = s // tm
    row = lambda i: (i, 0)
    col = lambda i: (i // nt, 0, i % nt)
    return pl.pallas_call(
        _rpa_proj_kernel,
        grid=(m // tm,),
        in_specs=[pl.BlockSpec((tm, d), row), _const_spec(w_qt.shape), _const_spec(w_k.shape),
                  _const_spec(w_vt.shape)],
        out_specs=[pl.BlockSpec((1, d, tm), col), pl.BlockSpec((tm, d), row),
                   pl.BlockSpec((1, RPA_PAIRS * RPA_VROWS, tm), col)],
        out_shape=[jax.ShapeDtypeStruct((b, d, s), BF16), jax.ShapeDtypeStruct((m, d), BF16),
                   jax.ShapeDtypeStruct((b, RPA_PAIRS * RPA_VROWS, s), BF16)],
        compiler_params=_params(("parallel",)),
        name="rpa_proj",
    )(h, w_qt, w_k, w_vt)


def _rpa_core_kernel(qt_ref, kp_ref, kc_ref, vtp_ref, vtc_ref, bias_ref, o_ref, kw_ref, vw_ref, st_ref, mx_ref):
    t = pl.program_id(1)
    tq = o_ref.shape[1]
    kw_ref[0:tq, :] = kp_ref[0]
    kw_ref[tq:2 * tq, :] = kc_ref[0]
    vw_ref[:, 0:tq] = vtp_ref[0]
    vw_ref[:, tq:2 * tq] = vtc_ref[0]

    first_head = lax.broadcasted_iota(jnp.int32, (LANES, RPA_GQ), 0) < RPA_HEAD_DIM
    key_row = lax.broadcasted_iota(jnp.int32, (RPA_GK, 2 * RPA_GQ), 0)

    def scores(slot, g, p):
        q0 = g * RPA_GQ
        feat = slice(p * LANES, (p + 1) * LANES)
        qt = qt_ref[0, feat, q0:q0 + RPA_GQ]
        zero = jnp.zeros_like(qt)
        q_blk = jnp.concatenate([jnp.where(first_head, qt, zero), jnp.where(first_head, zero, qt)], axis=1)
        st = _dot(kw_ref[q0:q0 + RPA_GK, feat], q_blk) + bias_ref[p]
        st = jnp.where(jnp.logical_and(t == 0, key_row + q0 < tq), NEG_INF, st)
        st_ref[slot] = st
        mx_ref[slot] = jnp.max(st, axis=0, keepdims=True)

    def consume(slot, g, p):
        q0 = g * RPA_GQ
        feat = slice(p * LANES, (p + 1) * LANES)
        e = jnp.exp2(st_ref[slot] - mx_ref[slot]).astype(BF16)
        ot = _dot(vw_ref[p * RPA_VROWS:(p + 1) * RPA_VROWS, q0:q0 + RPA_GK], e)
        on = ot[0:LANES, :] * (1.0 / ot[LANES:LANES + 1, :])
        pair_t = jnp.where(first_head, on[:, 0:RPA_GQ], on[:, RPA_GQ:2 * RPA_GQ])
        o_ref[0, q0:q0 + RPA_GQ, feat] = pair_t.T.astype(BF16)

    units = [(g, p) for g in range(tq // RPA_GQ) for p in range(RPA_PAIRS)]
    scores(0, *units[0])
    for i, unit in enumerate(units):
        if i + 1 < len(units):
            scores((i + 1) % 2, *units[i + 1])
        consume(i % 2, *unit)


def _rpa_core(qt, k, vt, bias_t):
    b, s, d = k.shape
    tq = RPA_TQ
    vrows = vt.shape[1]
    prev = lambda ti: jnp.maximum(ti - 1, 0)
    return pl.pallas_call(
        _rpa_core_kernel,
        grid=(b, s // tq),
        in_specs=[
            pl.BlockSpec((1, d, tq), lambda bi, ti: (bi, 0, ti)),
            pl.BlockSpec((1, tq, d), lambda bi, ti: (bi, prev(ti), 0)),
            pl.BlockSpec((1, tq, d), lambda bi, ti: (bi, ti, 0)),
            pl.BlockSpec((1, vrows, tq), lambda bi, ti: (bi, 0, prev(ti))),
            pl.BlockSpec((1, vrows, tq), lambda bi, ti: (bi, 0, ti)),
            _const_spec(bias_t.shape),
        ],
        out_specs=pl.BlockSpec((1, tq, d), lambda bi, ti: (bi, ti, 0)),
        out_shape=jax.ShapeDtypeStruct((b, s, d), BF16),
        scratch_shapes=[pltpu.VMEM((2 * tq, d), BF16), pltpu.VMEM((vrows, 2 * tq), BF16),
                        pltpu.VMEM((2, RPA_GK, 2 * RPA_GQ), F32), pltpu.VMEM((2, 1, 2 * RPA_GQ), F32)],
        compiler_params=_params(("parallel", "parallel")),
        name="rpa_core",
    )(qt, k, k, vt, vt, bias_t)


def _rpa_group_bias_t(rel_bias):
    gq, gk = RPA_GQ, RPA_GK
    span = gq + gk - 1
    diag = np.arange(span)
    rel = np.clip(RPA_LEFT_CHUNKS * CHUNK + gq - 1 - diag, -RPA_MAX_REL, RPA_MAX_REL) + RPA_MAX_REL
    u = jnp.pad(rel_bias[:, rel].astype(F32), ((0, 0), (0, 1)))
    shifted = jnp.tile(u, (1, gq))[:, :gq * span].reshape(-1, gq, span)
    table = shifted[:, :, gq - 1:gq - 1 + gk]
    r = np.arange(gq)[:, None]
    w = np.arange(gk)[None, :]
    band = w - (r // CHUNK) * CHUNK
    in_band = (band >= 0) & (band < (RPA_LEFT_CHUNKS + 1) * CHUNK)
    table = jnp.where(in_band[None], table * LOG2E, NEG_INF)
    table_t = table.transpose(0, 2, 1).reshape(RPA_PAIRS, 2, gk, gq)
    return table_t.transpose(0, 2, 1, 3).reshape(RPA_PAIRS, gk, 2 * gq)


def _rpa_layer(h, b, s, w_in, rel_bias):
    d = h.shape[1]
    w = w_in.astype(BF16)
    qt, k, vt = _rpa_proj(h, b, s, w[:, 0:d].T, w[:, d:2 * d], w[:, 2 * d:3 * d].T)
    o = _rpa_core(qt, k.reshape(b, s, d), vt, _rpa_group_bias_t(rel_bias))
    return o.reshape(b * s, d)


def _rms_norm(x, g):
    return x * lax.rsqrt(jnp.mean(x * x, axis=-1, keepdims=True) + RMS_EPS) * g


def _mla_proj_kernel(h_ref, cos_ref, sin_ref, cost_ref, sint_ref, wcq_ref, wckv_ref, wkr_ref, wkrs_ref,
                     gq_ref, gkv_ref, wqt_ref, wqst_ref, wkn_ref, wvt_ref,
                     qt_ref, kn_ref, kr_ref, vt_ref):
    xb = h_ref[...].astype(BF16)
    tm = h_ref.shape[0]
    scale = MLA_QK ** -0.5 * LOG2E
    cq = _rms_norm(_dot(xb, wcq_ref[...]), gq_ref[...]).astype(BF16)
    ckv = _rms_norm(_dot(xb, wckv_ref[...]), gkv_ref[...]).astype(BF16)
    kn_ref[...] = _dot(ckv, wkn_ref[...]).astype(BF16)
    kr = _dot(xb, wkr_ref[...]) * cos_ref[...] + _dot(xb, wkrs_ref[...]) * sin_ref[...]
    kr_ref[...] = kr.astype(BF16)
    vt = _dot_nt(wvt_ref[...], ckv).astype(BF16)
    for h in range(MLA_HEADS):
        r0 = h * MLA_VROWS
        vt_ref[0, 0, r0:r0 + MLA_V, :] = vt[h * MLA_V:(h + 1) * MLA_V]
        vt_ref[0, 0, r0 + MLA_V:r0 + MLA_VROWS, :] = jnp.ones((BF16_SUBLANES, tm), BF16)
    a = _dot_nt(wqt_ref[...], cq)
    bsw = _dot_nt(wqst_ref[...], cq)
    cos_t = cost_ref[...]
    sin_t = sint_ref[...]
    hq = 2 * LANES
    for h in range(MLA_HEADS):
        r0 = h * hq
        qt_ref[0, 0, r0:r0 + MLA_NOPE, :] = (a[r0:r0 + MLA_NOPE] * scale).astype(BF16)
        rope = a[r0 + MLA_NOPE:r0 + MLA_QK] * cos_t + bsw[h * MLA_ROPE:(h + 1) * MLA_ROPE] * sin_t
        qt_ref[0, 0, r0 + MLA_NOPE:r0 + MLA_QK, :] = (rope * scale).astype(BF16)
        qt_ref[0, 0, r0 + MLA_QK:r0 + hq, :] = jnp.zeros((hq - MLA_QK, qt_ref.shape[3]), BF16)


def _mla_proj(h, b, s, tables, weights):
    m, d = h.shape
    tm = MLA_TK
    nt = s // tm
    row = lambda i: (i, 0)
    pos = lambda i: (i % nt, 0)
    pos_t = lambda i: (0, i % nt)
    tile_t = lambda i: (i // nt, i % nt, 0, 0)
    wide = MLA_HEADS * LANES
    return pl.pallas_call(
        _mla_proj_kernel,
        grid=(m // tm,),
        in_specs=[pl.BlockSpec((tm, d), row), pl.BlockSpec((tm, LANES), pos), pl.BlockSpec((tm, LANES), pos),
                  pl.BlockSpec((MLA_ROPE, tm), pos_t), pl.BlockSpec((MLA_ROPE, tm), pos_t)]
        + [_const_spec(w.shape) for w in weights],
        out_specs=[pl.BlockSpec((1, 1, 2 * wide, tm), tile_t), pl.BlockSpec((tm, wide), row),
                   pl.BlockSpec((tm, LANES), row), pl.BlockSpec((1, 1, MLA_HEADS * MLA_VROWS, tm), tile_t)],
        out_shape=[jax.ShapeDtypeStruct((b, nt, 2 * wide, tm), BF16), jax.ShapeDtypeStruct((m, wide), BF16),
                   jax.ShapeDtypeStruct((m, LANES), BF16),
                   jax.ShapeDtypeStruct((b, nt, MLA_HEADS * MLA_VROWS, tm), BF16)],
        compiler_params=_params(("parallel",)),
        name="mla_proj",
    )(h, *tables, *weights)


def _mla_attn_kernel(qt_ref, kn_ref, kr_ref, vt_ref, o_ref, m_ref, acc_ref, st_ref, mx_ref):
    qi = pl.program_id(2)
    tq, tk = MLA_TQ, MLA_TK
    assert tq == 2 * tk
    m_ref[...] = jnp.full_like(m_ref, NEG_INF)
    acc_ref[...] = jnp.zeros_like(acc_ref)

    def scores(slot, j):
        rows = pl.ds(pl.multiple_of(j * tk, tk), tk)
        kcat = jnp.concatenate([kn_ref[0, rows, :], kr_ref[0, rows, :]], axis=1)
        st_ref[slot, :, 0:tk] = _dot(kcat, qt_ref[0, 0])
        st_ref[slot, :, tk:tq] = _dot(kcat, qt_ref[0, 1])
        mx_ref[slot] = jnp.max(st_ref[slot], axis=0, keepdims=True)

    def consume(slot, j, first_key_chunk=None):
        st = st_ref[slot]
        if first_key_chunk is None:
            mx = mx_ref[slot]
        else:
            kchunk = first_key_chunk + lax.broadcasted_iota(jnp.int32, (tk, tq), 0) // CHUNK
            qchunk = lax.broadcasted_iota(jnp.int32, (tk, tq), 1) // CHUNK
            st = jnp.where(kchunk <= qchunk, st, NEG_INF)
            mx = jnp.max(st, axis=0, keepdims=True)
        m_prev = m_ref[...]
        m_new = jnp.maximum(m_prev, mx)
        alpha = jnp.exp2(m_prev - m_new)
        p = jnp.exp2(st - m_new).astype(BF16)
        acc_ref[...] = alpha * acc_ref[...] + _dot(vt_ref[0, j], p)
        m_ref[...] = m_new

    scores(0, 0)

    def pair(i, carry):
        j = 2 * i
        scores(1, j + 1)
        consume(0, j)
        scores(0, j + 2)
        consume(1, j + 1)
        return carry

    lax.fori_loop(0, qi, pair, 0)
    scores(1, 2 * qi + 1)
    consume(0, 2 * qi, first_key_chunk=0)
    consume(1, 2 * qi + 1, first_key_chunk=tk // CHUNK)

    acc = acc_ref[...]
    o_ref[0] = (acc[0:MLA_V] * (1.0 / acc[MLA_V:MLA_V + 1])).T.astype(BF16)


def _mla_attn(qt, kn, kr, vt):
    b, nt, _, tk = qt.shape
    s = kn.shape[1]
    tq = MLA_TQ
    assert MLA_TK == tk
    return pl.pallas_call(
        _mla_attn_kernel,
        grid=(b, MLA_HEADS, s // tq),
        in_specs=[
            pl.BlockSpec((1, tq // tk, 2 * LANES, tk), lambda bi, hi, qi: (bi, qi, hi, 0)),
            pl.BlockSpec((1, s, LANES), lambda bi, hi, qi: (bi, 0, hi)),
            pl.BlockSpec((1, s, LANES), lambda bi, hi, qi: (bi, 0, 0)),
            pl.BlockSpec((1, nt, MLA_VROWS, tk), lambda bi, hi, qi: (bi, 0, hi, 0)),
        ],
        out_specs=pl.BlockSpec((1, tq, LANES), lambda bi, hi, qi: (bi, qi, hi)),
        out_shape=jax.ShapeDtypeStruct((b, s, MLA_HEADS * MLA_V), BF16),
        scratch_shapes=[
            pltpu.VMEM((1, tq), F32),
            pltpu.VMEM((MLA_VROWS, tq), F32),
            pltpu.VMEM((2, tk, tq), F32),
            pltpu.VMEM((2, 1, tq), F32),
        ],
        compiler_params=_params(("parallel", "parallel", "arbitrary")),
        name="mla_attn",
    )(qt, kn, kr, vt)


def _swap_halves(w):
    half = w.shape[-1] // 2
    return jnp.concatenate([w[..., half:], w[..., :half]], axis=-1)


def _mla_layer(h, b, s, w_in, q_norm_g, kv_norm_g, w_uq, w_ukv):
    pad = LANES - MLA_ROPE
    w_cq = w_in[:, :MLA_Q_RANK]
    w_ckv = w_in[:, MLA_Q_RANK:MLA_Q_RANK + MLA_KV_RANK]
    w_kr = w_in[:, MLA_Q_RANK + MLA_KV_RANK:]
    pad_cols = lambda w: jnp.pad(w, [(0, 0)] * (w.ndim - 1) + [(0, pad)])
    w_uq = w_uq.reshape(MLA_Q_RANK, MLA_HEADS, MLA_QK)
    w_qt = jnp.pad(w_uq, [(0, 0), (0, 0), (0, 2 * LANES - MLA_QK)]).reshape(MLA_Q_RANK, -1).T
    w_qst = _swap_halves(w_uq[:, :, MLA_NOPE:]).reshape(MLA_Q_RANK, -1).T
    w_ukv = w_ukv.reshape(MLA_KV_RANK, MLA_HEADS, MLA_NOPE + MLA_V)
    w_kn = w_ukv[:, :, :MLA_NOPE].reshape(MLA_KV_RANK, MLA_HEADS * MLA_NOPE)
    w_vt = w_ukv[:, :, MLA_NOPE:].reshape(MLA_KV_RANK, MLA_HEADS * MLA_V).T
    inv = ROPE_BASE ** (-jnp.arange(0, MLA_ROPE, 2, dtype=F32) / MLA_ROPE)
    ang = jnp.arange(s, dtype=F32)[:, None] * inv[None, :]
    cos, sin = jnp.cos(ang), jnp.sin(ang)
    cos2 = jnp.concatenate([cos, cos], axis=1)
    sin2 = jnp.concatenate([-sin, sin], axis=1)
    tables = [jnp.concatenate([cos2, cos2], axis=1), jnp.concatenate([sin2, sin2], axis=1), cos2.T, sin2.T]
    bf = lambda w: w.astype(BF16)
    weights = [bf(w_cq), bf(w_ckv), bf(pad_cols(w_kr)), bf(pad_cols(_swap_halves(w_kr))),
               q_norm_g.reshape(1, -1), kv_norm_g.reshape(1, -1),
               bf(w_qt), bf(w_qst), bf(w_kn), bf(w_vt)]
    qt, kn, kr, vt = _mla_proj(h, b, s, tables, weights)
    sh = lambda t: t.reshape(b, s, t.shape[-1])
    o = _mla_attn(qt, sh(kn), sh(kr), vt)
    return o.reshape(b * s, MLA_HEADS * MLA_V)


def kernel(x, gla_w_in, gla_w_gate_up, gla_b_gate, gla_norm_g, gla_w_out,
           rpa_w_in, rpa_rel_bias, rpa_w_out,
           mla_w_in, mla_q_norm_g, mla_kv_norm_g, mla_w_uq, mla_w_ukv, mla_w_out,
           ffn_w_in, ffn_w_out, ln_g, ln_b):
    b, s, d = x.shape
    h = x.reshape(b * s, d)
    assert ffn_w_out.shape[1] % FFN_CHUNK == 0
    for i in range(DEPTH):
        m, j = i % N_MIXERS, i // N_MIXERS
        if m == 0:
            o = _gla_layer(h, b, s, gla_w_in[j], gla_w_gate_up[j], gla_b_gate[j], gla_norm_g[j])
            w_o = gla_w_out[j]
        elif m == 1:
            o = _rpa_layer(h, b, s, rpa_w_in[j], rpa_rel_bias[j])
            w_o = rpa_w_out[j]
        else:
            o = _mla_layer(h, b, s, mla_w_in[j], mla_q_norm_g[j], mla_kv_norm_g[j], mla_w_uq[j], mla_w_ukv[j])
            w_o = mla_w_out[j]
        ln = jnp.stack([ln_g[i, 0], ln_b[i, 0], ln_g[i, 1], ln_b[i, 1]])
        h = _post(o, h, w_o.astype(BF16), ln, ffn_w_in[i].astype(BF16), ffn_w_out[i].astype(BF16))
    return h.reshape(b, s, d)
```

```python
import functools

import jax
import jax.numpy as jnp
import numpy as np
from jax import lax
from jax.experimental import pallas as pl
from jax.experimental.pallas import tpu as pltpu

F32 = jnp.float32
BF16 = jnp.bfloat16

DEPTH = 4
N_MIXERS = 3
CHUNK = 64
DEEPNORM_ALPHA = (2 * DEPTH) ** 0.25
LN_EPS = 1e-5
RMS_EPS = 1e-6
NEG_INF = -1e30

GLA_HEADS = 4
GLA_DK = 128
GLA_DV = 256
GLA_GATE_RANK = 16
GLA_TAU = 16.0
GLA_HK = GLA_HEADS * GLA_DK
GLA_HV = GLA_HEADS * GLA_DV
GLA_BLOCK = 512

RPA_HEADS = 16
RPA_HEAD_DIM = 64
RPA_LEFT_CHUNKS = 8
RPA_MAX_REL = 128
RPA_GROUP = 2
RPA_GQ = RPA_GROUP * CHUNK
RPA_GK = (RPA_LEFT_CHUNKS + RPA_GROUP) * CHUNK
RPA_TQ = 512
RPA_PAIRS = RPA_HEADS // 2

MLA_HEADS = 8
MLA_NOPE = 128
MLA_ROPE = 64
MLA_V = 128
MLA_Q_RANK = 384
MLA_KV_RANK = 256
MLA_QK = MLA_NOPE + MLA_ROPE
ROPE_BASE = 10000.0
MLA_TQ = 1024
MLA_TK = 512
BF16_SUBLANES = 16
MLA_VROWS = MLA_V + BF16_SUBLANES
LOG2E = 1.4426950408889634
RPA_VROWS = 128 + BF16_SUBLANES

LANES = 128
VMEM_LIMIT = 56 * 1024 * 1024

TOKEN_TILE = 512
POST_SUBTILES = 2
FFN_CHUNK = 256


def _params(sem):
    return pltpu.CompilerParams(dimension_semantics=sem, vmem_limit_bytes=VMEM_LIMIT)


def _const_spec(shape):
    nd = len(shape)
    return pl.BlockSpec(shape, lambda *_: (0,) * nd, pipeline_mode=pl.Buffered(1))


def _layer_norm(x, g, b):
    mu = jnp.mean(x, axis=-1, keepdims=True)
    xc = x - mu
    var = jnp.mean(xc * xc, axis=-1, keepdims=True)
    return xc * lax.rsqrt(var + LN_EPS) * g + b


def _dot(a, b):
    return jnp.dot(a, b, preferred_element_type=F32)


def _dot_nt(a, b):
    return lax.dot_general(a, b, (((1,), (1,)), ((), ())), preferred_element_type=F32)


def _dot_tn(a, b):
    return lax.dot_general(a, b, (((0,), (0,)), ((), ())), preferred_element_type=F32)


def _post_kernel(o_ref, h_ref, wo_ref, ln_ref, wgu_ref, wdn_ref, out_ref, xb_ref):
    tm = out_ref.shape[0]
    hidden = wdn_ref.shape[0]
    halves = [slice(r0, r0 + TOKEN_TILE) for r0 in range(0, tm, TOKEN_TILE)]

    def mixer_out_norm(rows):
        y = _dot(o_ref[rows, :], wo_ref[...])
        h1 = _layer_norm(DEEPNORM_ALPHA * h_ref[rows, :] + y, ln_ref[0:1, :], ln_ref[1:2, :])
        out_ref[rows, :] = h1
        xb_ref[rows, :] = h1.astype(BF16)

    def ffn_norm(rows):
        xb = xb_ref[rows, :]
        acc = None
        for c0 in range(0, hidden, FFN_CHUNK):
            g = _dot(xb, wgu_ref[:, c0:c0 + FFN_CHUNK])
            u = _dot(xb, wgu_ref[:, hidden + c0:hidden + c0 + FFN_CHUNK])
            a = (g * jax.nn.sigmoid(g) * u).astype(BF16)
            part = _dot(a, wdn_ref[c0:c0 + FFN_CHUNK, :])
            acc = part if acc is None else acc + part
        out_ref[rows, :] = _layer_norm(DEEPNORM_ALPHA * out_ref[rows, :] + acc, ln_ref[2:3, :], ln_ref[3:4, :])

    for rows in halves:
        mixer_out_norm(rows)
    for rows in halves:
        ffn_norm(rows)


def _post(o, h, w_o, ln, w_gu, w_dn):
    m, d = h.shape
    tm = POST_SUBTILES * TOKEN_TILE
    row = lambda i: (i, 0)
    return pl.pallas_call(
        _post_kernel,
        grid=(m // tm,),
        in_specs=[
            pl.BlockSpec((tm, o.shape[1]), row),
            pl.BlockSpec((tm, d), row),
            _const_spec(w_o.shape),
            _const_spec(ln.shape),
            _const_spec(w_gu.shape),
            _const_spec(w_dn.shape),
        ],
        out_specs=pl.BlockSpec((tm, d), row),
        out_shape=jax.ShapeDtypeStruct((m, d), F32),
        scratch_shapes=[pltpu.VMEM((tm, d), BF16)],
        compiler_params=_params(("parallel",)),
        name="post_ffn",
    )(o, h, w_o, ln, w_gu, w_dn)


def _gla_proj_kernel(h_ref, wm_ref, wgl_ref, wgu_ref, bg_ref, qk_ref, v_ref, r_ref, la_ref):
    xb = h_ref[...].astype(BF16)
    g_low = _dot(xb, wgl_ref[...])
    z = _dot(g_low.astype(BF16), wgu_ref[...]) + bg_ref[...]
    log_sig = jnp.minimum(z, 0.0) - jnp.log(1.0 + jnp.exp(-jnp.abs(z)))
    la_ref[...] = log_sig / GLA_TAU
    qk_ref[...] = _dot(xb, wm_ref[:, 0:2 * GLA_HK])
    v_ref[...] = _dot(xb, wm_ref[:, 2 * GLA_HK:2 * GLA_HK + GLA_HV]).astype(BF16)
    r_ref[...] = _dot(xb, wm_ref[:, 2 * GLA_HK + GLA_HV:])


def _gla_proj(h, w_main, w_gl, w_gu, b_gate):
    m, d = h.shape
    tm = TOKEN_TILE
    row = lambda i: (i, 0)
    return pl.pallas_call(
        _gla_proj_kernel,
        grid=(m // tm,),
        in_specs=[
            pl.BlockSpec((tm, d), row),
            _const_spec(w_main.shape),
            _const_spec(w_gl.shape),
            _const_spec(w_gu.shape),
            _const_spec(b_gate.shape),
        ],
        out_specs=[
            pl.BlockSpec((tm, 2 * GLA_HK), row),
            pl.BlockSpec((tm, GLA_HV), row),
            pl.BlockSpec((tm, GLA_HV), row),
            pl.BlockSpec((tm, GLA_HK), row),
        ],
        out_shape=[
            jax.ShapeDtypeStruct((m, 2 * GLA_HK), F32),
            jax.ShapeDtypeStruct((m, GLA_HV), BF16),
            jax.ShapeDtypeStruct((m, GLA_HV), F32),
            jax.ShapeDtypeStruct((m, GLA_HK), F32),
        ],
        compiler_params=_params(("parallel",)),
        name="gla_proj",
    )(h, w_main, w_gl, w_gu, b_gate)


def _split3_bf16(x):
    hi = x.astype(BF16)
    r1 = x - hi.astype(F32)
    mid = r1.astype(BF16)
    lo = (r1 - mid.astype(F32)).astype(BF16)
    return hi, mid, lo


def _gla_core_kernel(qk_ref, v_ref, r_ref, la_ref, g_ref, y_ref, state_ref, *, n_chunks):
    @pl.when(pl.program_id(1) == 0)
    def _():
        state_ref[...] = jnp.zeros_like(state_ref)

    row_i = lax.broadcasted_iota(jnp.int32, (CHUNK, CHUNK), 0)
    col_i = lax.broadcasted_iota(jnp.int32, (CHUNK, CHUNK), 1)
    causal = col_i <= row_i
    tri = causal.astype(BF16)

    def prepare(ci):
        rows = slice(ci * CHUNK, (ci + 1) * CHUNK)
        la = la_ref[0, rows, :]
        hi, mid, lo = _split3_bf16(la)
        cum = _dot(tri, hi) + _dot(tri, mid) + _dot(tri, lo)
        tot = cum[CHUNK - 1:CHUNK, :]
        q = qk_ref[0, rows, 0:GLA_HK] * (GLA_DK ** -0.5)
        k = qk_ref[0, rows, GLA_HK:2 * GLA_HK]
        q_dec = (q * jnp.exp(cum)).astype(BF16)
        k_inv = (k * jnp.exp(-cum)).astype(BF16)
        k_end = (k * jnp.exp(tot - cum)).astype(BF16)
        decay = jnp.exp(tot)
        o_intra = []
        for h in range(GLA_HEADS):
            kc = slice(h * GLA_DK, (h + 1) * GLA_DK)
            vc = slice(h * GLA_DV, (h + 1) * GLA_DV)
            a = jnp.where(causal, _dot_nt(q_dec[:, kc], k_inv[:, kc]), 0.0)
            o_intra.append(_dot(a.astype(BF16), v_ref[0, rows, vc]))
        return q_dec, k_end, decay, o_intra

    def recur(ci, prepared):
        rows = slice(ci * CHUNK, (ci + 1) * CHUNK)
        q_dec, k_end, decay, o_intra = prepared
        for h in range(GLA_HEADS):
            kc = slice(h * GLA_DK, (h + 1) * GLA_DK)
            vc = slice(h * GLA_DV, (h + 1) * GLA_DV)
            vh = v_ref[0, rows, vc]
            st = state_ref[h]
            o = o_intra[h] + _dot_nt(q_dec[:, kc], st.astype(BF16))
            state_ref[h] = st * decay[:, kc] + _dot_tn(vh, k_end[:, kc])
            mu = jnp.mean(o, axis=-1, keepdims=True)
            oc = o - mu
            var = jnp.mean(oc * oc, axis=-1, keepdims=True)
            on = oc * lax.rsqrt(var + LN_EPS) * g_ref[:, vc]
            r = r_ref[0, rows, vc]
            y_ref[0, rows, vc] = (r * jax.nn.sigmoid(r) * on).astype(BF16)

    prepared = prepare(0)
    for ci in range(n_chunks):
        upcoming = prepare(ci + 1) if ci + 1 < n_chunks else None
        recur(ci, prepared)
        prepared = upcoming


def _gla_core(qk, v, r, la, norm_g, *, block):
    b, s, _ = qk.shape
    blk = lambda bi, ti: (bi, ti, 0)
    return pl.pallas_call(
        functools.partial(_gla_core_kernel, n_chunks=block // CHUNK),
        grid=(b, s // block),
        in_specs=[
            pl.BlockSpec((1, block, 2 * GLA_HK), blk),
            pl.BlockSpec((1, block, GLA_HV), blk),
            pl.BlockSpec((1, block, GLA_HV), blk),
            pl.BlockSpec((1, block, GLA_HK), blk),
            pl.BlockSpec((1, GLA_HV), lambda bi, ti: (0, 0)),
        ],
        out_specs=pl.BlockSpec((1, block, GLA_HV), blk),
        out_shape=jax.ShapeDtypeStruct((b, s, GLA_HV), BF16),
        scratch_shapes=[pltpu.VMEM((GLA_HEADS, GLA_DV, GLA_DK), F32)],
        compiler_params=_params(("parallel", "arbitrary")),
        name="gla_core",
    )(qk, v, r, la, norm_g)


def _gla_layer(h, b, s, w_in, w_gate_up, b_gate, norm_g):
    d = h.shape[1]
    n_main = 2 * GLA_HK + 2 * GLA_HV
    w_main = w_in[:, :n_main].astype(BF16)
    w_gl = jnp.zeros((d, LANES), BF16).at[:, :GLA_GATE_RANK].set(w_in[:, n_main:].astype(BF16))
    w_gu = jnp.zeros((LANES, GLA_HK), BF16).at[:GLA_GATE_RANK, :].set(w_gate_up.astype(BF16))
    qk, v, r, la = _gla_proj(h, w_main, w_gl, w_gu, b_gate.reshape(1, GLA_HK))
    sh = lambda t: t.reshape(b, s, t.shape[-1])
    y = _gla_core(sh(qk), sh(v), sh(r), sh(la), norm_g.reshape(1, GLA_HV), block=GLA_BLOCK)
    return y.reshape(b * s, GLA_HV)


def _rpa_proj_kernel(h_ref, wqt_ref, wk_ref, wvt_ref, qt_ref, k_ref, vt_ref):
    xb = h_ref[...].astype(BF16)
    tm = h_ref.shape[0]
    qt_ref[0] = (_dot_nt(wqt_ref[...], xb) * (RPA_HEAD_DIM ** -0.5 * LOG2E)).astype(BF16)
    k_ref[...] = _dot(xb, wk_ref[...]).astype(BF16)
    vt = _dot_nt(wvt_ref[...], xb).astype(BF16)
    for p in range(RPA_PAIRS):
        r0 = p * RPA_VROWS
        vt_ref[0, r0:r0 + LANES, :] = vt[p * LANES:(p + 1) * LANES]
        vt_ref[0, r0 + LANES:r0 + RPA_VROWS, :] = jnp.ones((BF16_SUBLANES, tm), BF16)


def _rpa_proj(h, b, s, w_qt, w_k, w_vt):
    m, d = h.shape
    tm = RPA_TQ
    nt = s // tm
    row = lambda i: (i, 0)
    col = lambda i: (i // nt, 0, i % nt)
    return pl.pallas_call(
        _rpa_proj_kernel,
        grid=(m // tm,),
        in_specs=[pl.BlockSpec((tm, d), row), _const_spec(w_qt.shape), _const_spec(w_k.shape),
                  _const_spec(w_vt.shape)],
        out_specs=[pl.BlockSpec((1, d, tm), col), pl.BlockSpec((tm, d), row),
                   pl.BlockSpec((1, RPA_PAIRS * RPA_VROWS, tm), col)],
        out_shape=[jax.ShapeDtypeStruct((b, d, s), BF16), jax.ShapeDtypeStruct((m, d), BF16),
                   jax.ShapeDtypeStruct((b, RPA_PAIRS * RPA_VROWS, s), BF16)],
        compiler_params=_params(("parallel",)),
        name="rpa_proj",
    )(h, w_qt, w_k, w_vt)


def _rpa_core_kernel(qt_ref, kp_ref, kc_ref, vtp_ref, vtc_ref, bias_ref, o_ref, kw_ref, vw_ref, st_ref, mx_ref,
                     e_ref):
    t = pl.program_id(1)
    tq = o_ref.shape[1]
    kw_ref[0:tq, :] = kp_ref[0]
    kw_ref[tq:2 * tq, :] = kc_ref[0]
    vw_ref[:, 0:tq] = vtp_ref[0]
    vw_ref[:, tq:2 * tq] = vtc_ref[0]

    first_head = lax.broadcasted_iota(jnp.int32, (LANES, RPA_GQ), 0) < RPA_HEAD_DIM
    key_row = lax.broadcasted_iota(jnp.int32, (RPA_GK, 2 * RPA_GQ), 0)

    def scores(slot, g, p):
        q0 = g * RPA_GQ
        feat = slice(p * LANES, (p + 1) * LANES)
        qt = qt_ref[0, feat, q0:q0 + RPA_GQ]
        zero = jnp.zeros_like(qt)
        q_blk = jnp.concatenate([jnp.where(first_head, qt, zero), jnp.where(first_head, zero, qt)], axis=1)
        st = _dot(kw_ref[q0:q0 + RPA_GK, feat], q_blk) + bias_ref[p]
        st = jnp.where(jnp.logical_and(t == 0, key_row + q0 < tq), NEG_INF, st)
        st_ref[slot] = st
        mx_ref[slot] = jnp.max(st, axis=0, keepdims=True)

    def softmax(slot, g, p):
        e_ref[slot] = jnp.exp2(st_ref[slot] - mx_ref[slot]).astype(BF16)

    def values(slot, g, p):
        q0 = g * RPA_GQ
        feat = slice(p * LANES, (p + 1) * LANES)
        ot = _dot(vw_ref[p * RPA_VROWS:(p + 1) * RPA_VROWS, q0:q0 + RPA_GK], e_ref[slot])
        on = ot[0:LANES, :] * (1.0 / ot[LANES:LANES + 1, :])
        pair_t = jnp.where(first_head, on[:, 0:RPA_GQ], on[:, RPA_GQ:2 * RPA_GQ])
        o_ref[0, q0:q0 + RPA_GQ, feat] = pair_t.T.astype(BF16)

    units = [(g, p) for g in range(tq // RPA_GQ) for p in range(RPA_PAIRS)]
    n = len(units)
    scores(0, *units[0])
    scores(1, *units[1])
    softmax(0, *units[0])
    for i in range(n):
        if i + 2 < n:
            scores(i % 2, *units[i + 2])
        if i + 1 < n:
            softmax((i + 1) % 2, *units[i + 1])
        values(i % 2, *units[i])


def _rpa_core(qt, k, vt, bias_t):
    b, s, d = k.shape
    tq = RPA_TQ
    vrows = vt.shape[1]
    prev = lambda ti: jnp.maximum(ti - 1, 0)
    return pl.pallas_call(
        _rpa_core_kernel,
        grid=(b, s // tq),
        in_specs=[
            pl.BlockSpec((1, d, tq), lambda bi, ti: (bi, 0, ti)),
            pl.BlockSpec((1, tq, d), lambda bi, ti: (bi, prev(ti), 0)),
            pl.BlockSpec((1, tq, d), lambda bi, ti: (bi, ti, 0)),
            pl.BlockSpec((1, vrows, tq), lambda bi, ti: (bi, 0, prev(ti))),
            pl.BlockSpec((1, vrows, tq), lambda bi, ti: (bi, 0, ti)),
            _const_spec(bias_t.shape),
        ],
        out_specs=pl.BlockSpec((1, tq, d), lambda bi, ti: (bi, ti, 0)),
        out_shape=jax.ShapeDtypeStruct((b, s, d), BF16),
        scratch_shapes=[pltpu.VMEM((2 * tq, d), BF16), pltpu.VMEM((vrows, 2 * tq), BF16),
                        pltpu.VMEM((2, RPA_GK, 2 * RPA_GQ), F32), pltpu.VMEM((2, 1, 2 * RPA_GQ), F32),
                        pltpu.VMEM((2, RPA_GK, 2 * RPA_GQ), BF16)],
        compiler_params=_params(("parallel", "parallel")),
        name="rpa_core",
    )(qt, k, k, vt, vt, bias_t)


def _rpa_group_bias_t(rel_bias):
    gq, gk = RPA_GQ, RPA_GK
    span = gq + gk - 1
    diag = np.arange(span)
    rel = np.clip(RPA_LEFT_CHUNKS * CHUNK + gq - 1 - diag, -RPA_MAX_REL, RPA_MAX_REL) + RPA_MAX_REL
    u = jnp.pad(rel_bias[:, rel].astype(F32), ((0, 0), (0, 1)))
    shifted = jnp.tile(u, (1, gq))[:, :gq * span].reshape(-1, gq, span)
    table = shifted[:, :, gq - 1:gq - 1 + gk]
    r = np.arange(gq)[:, None]
    w = np.arange(gk)[None, :]
    band = w - (r // CHUNK) * CHUNK
    in_band = (band >= 0) & (band < (RPA_LEFT_CHUNKS + 1) * CHUNK)
    table = jnp.where(in_band[None], table * LOG2E, NEG_INF)
    table_t = table.transpose(0, 2, 1).reshape(RPA_PAIRS, 2, gk, gq)
    return table_t.transpose(0, 2, 1, 3).reshape(RPA_PAIRS, gk, 2 * gq)


def _rpa_layer(h, b, s, w_in, rel_bias):
    d = h.shape[1]
    w = w_in.astype(BF16)
    qt, k, vt = _rpa_proj(h, b, s, w[:, 0:d].T, w[:, d:2 * d], w[:, 2 * d:3 * d].T)
    o = _rpa_core(qt, k.reshape(b, s, d), vt, _rpa_group_bias_t(rel_bias))
    return o.reshape(b * s, d)


def _rms_norm(x, g):
    return x * lax.rsqrt(jnp.mean(x * x, axis=-1, keepdims=True) + RMS_EPS) * g


def _mla_proj_kernel(h_ref, cos_ref, sin_ref, cost_ref, sint_ref, wcq_ref, wckv_ref, wkr_ref, wkrs_ref,
                     gq_ref, gkv_ref, wqt_ref, wqst_ref, wkn_ref, wvt_ref,
                     qt_ref, kn_ref, kr_ref, vt_ref):
    xb = h_ref[...].astype(BF16)
    tm = h_ref.shape[0]
    scale = MLA_QK ** -0.5 * LOG2E
    cq = _rms_norm(_dot(xb, wcq_ref[...]), gq_ref[...]).astype(BF16)
    ckv = _rms_norm(_dot(xb, wckv_ref[...]), gkv_ref[...]).astype(BF16)
    kn_ref[...] = _dot(ckv, wkn_ref[...]).astype(BF16)
    kr = _dot(xb, wkr_ref[...]) * cos_ref[...] + _dot(xb, wkrs_ref[...]) * sin_ref[...]
    kr_ref[...] = kr.astype(BF16)
    vt = _dot_nt(wvt_ref[...], ckv).astype(BF16)
    for h in range(MLA_HEADS):
        r0 = h * MLA_VROWS
        vt_ref[0, 0, r0:r0 + MLA_V, :] = vt[h * MLA_V:(h + 1) * MLA_V]
        vt_ref[0, 0, r0 + MLA_V:r0 + MLA_VROWS, :] = jnp.ones((BF16_SUBLANES, tm), BF16)
    a = _dot_nt(wqt_ref[...], cq)
    bsw = _dot_nt(wqst_ref[...], cq)
    cos_t = cost_ref[...]
    sin_t = sint_ref[...]
    hq = 2 * LANES
    for h in range(MLA_HEADS):
        r0 = h * hq
        qt_ref[0, 0, r0:r0 + MLA_NOPE, :] = (a[r0:r0 + MLA_NOPE] * scale).astype(BF16)
        rope = a[r0 + MLA_NOPE:r0 + MLA_QK] * cos_t + bsw[h * MLA_ROPE:(h + 1) * MLA_ROPE] * sin_t
        qt_ref[0, 0, r0 + MLA_NOPE:r0 + MLA_QK, :] = (rope * scale).astype(BF16)
        qt_ref[0, 0, r0 + MLA_QK:r0 + hq, :] = jnp.zeros((hq - MLA_QK, qt_ref.shape[3]), BF16)


def _mla_proj(h, b, s, tables, weights):
    m, d = h.shape
    tm = MLA_TK
    nt = s // tm
    row = lambda i: (i, 0)
    pos = lambda i: (i % nt, 0)
    pos_t = lambda i: (0, i % nt)
    tile_t = lambda i: (i // nt, i % nt, 0, 0)
    wide = MLA_HEADS * LANES
    return pl.pallas_call(
        _mla_proj_kernel,
        grid=(m // tm,),
        in_specs=[pl.BlockSpec((tm, d), row), pl.BlockSpec((tm, LANES), pos), pl.BlockSpec((tm, LANES), pos),
                  pl.BlockSpec((MLA_ROPE, tm), pos_t), pl.BlockSpec((MLA_ROPE, tm), pos_t)]
        + [_const_spec(w.shape) for w in weights],
        out_specs=[pl.BlockSpec((1, 1, 2 * wide, tm), tile_t), pl.BlockSpec((tm, wide), row),
                   pl.BlockSpec((tm, LANES), row), pl.BlockSpec((1, 1, MLA_HEADS * MLA_VROWS, tm), tile_t)],
        out_shape=[jax.ShapeDtypeStruct((b, nt, 2 * wide, tm), BF16), jax.ShapeDtypeStruct((m, wide), BF16),
                   jax.ShapeDtypeStruct((m, LANES), BF16),
                   jax.ShapeDtypeStruct((b, nt, MLA_HEADS * MLA_VROWS, tm), BF16)],
        compiler_params=_params(("parallel",)),
        name="mla_proj",
    )(h, *tables, *weights)


def _mla_attn_kernel(qt_ref, qnext_ref, kn_ref, kr_ref, vt_ref, o_ref, m_ref, acc_ref, st_ref, mx_ref):
    qi = pl.program_id(2)
    tq, tk = MLA_TQ, MLA_TK
    assert tq == 2 * tk
    m_ref[...] = jnp.full_like(m_ref, NEG_INF)
    acc_ref[...] = jnp.zeros_like(acc_ref)

    def scores(slot, j, q_ref=qt_ref):
        rows = pl.ds(pl.multiple_of(j * tk, tk), tk)
        kcat = jnp.concatenate([kn_ref[0, rows, :], kr_ref[0, rows, :]], axis=1)
        st_ref[slot, :, 0:tk] = _dot(kcat, q_ref[0, 0])
        st_ref[slot, :, tk:tq] = _dot(kcat, q_ref[0, 1])
        mx_ref[slot] = jnp.max(st_ref[slot], axis=0, keepdims=True)

    def consume(slot, j, first_key_chunk=None):
        st = st_ref[slot]
        if first_key_chunk is None:
            mx = mx_ref[slot]
        else:
            kchunk = first_key_chunk + lax.broadcasted_iota(jnp.int32, (tk, tq), 0) // CHUNK
            qchunk = lax.broadcasted_iota(jnp.int32, (tk, tq), 1) // CHUNK
            st = jnp.where(kchunk <= qchunk, st, NEG_INF)
            mx = jnp.max(st, axis=0, keepdims=True)
        m_prev = m_ref[...]
        m_new = jnp.maximum(m_prev, mx)
        alpha = jnp.exp2(m_prev - m_new)
        p = jnp.exp2(st - m_new).astype(BF16)
        acc_ref[...] = alpha * acc_ref[...] + _dot(vt_ref[0, j], p)
        m_ref[...] = m_new

    @pl.when(qi == 0)
    def _():
        scores(0, 0)

    def pair(i, carry):
        j = 2 * i
        scores(1, j + 1)
        consume(0, j)
        scores(0, j + 2)
        consume(1, j + 1)
        return carry

    lax.fori_loop(0, qi, pair, 0)
    scores(1, 2 * qi + 1)
    consume(0, 2 * qi, first_key_chunk=0)
    scores(0, 0, qnext_ref)
    consume(1, 2 * qi + 1, first_key_chunk=tk // CHUNK)

    acc = acc_ref[...]
    o_ref[0] = (acc[0:MLA_V] * (1.0 / acc[MLA_V:MLA_V + 1])).T.astype(BF16)


def _mla_attn(qt, kn, kr, vt):
    b, nt, _, tk = qt.shape
    s = kn.shape[1]
    tq = MLA_TQ
    assert MLA_TK == tk
    last_q = s // tq - 1
    return pl.pallas_call(
        _mla_attn_kernel,
        grid=(b, MLA_HEADS, s // tq),
        in_specs=[
            pl.BlockSpec((1, tq // tk, 2 * LANES, tk), lambda bi, hi, qi: (bi, qi, hi, 0)),
            pl.BlockSpec((1, tq // tk, 2 * LANES, tk), lambda bi, hi, qi: (bi, jnp.minimum(qi + 1, last_q), hi, 0)),
            pl.BlockSpec((1, s, LANES), lambda bi, hi, qi: (bi, 0, hi)),
            pl.BlockSpec((1, s, LANES), lambda bi, hi, qi: (bi, 0, 0)),
            pl.BlockSpec((1, nt, MLA_VROWS, tk), lambda bi, hi, qi: (bi, 0, hi, 0)),
        ],
        out_specs=pl.BlockSpec((1, tq, LANES), lambda bi, hi, qi: (bi, qi, hi)),
        out_shape=jax.ShapeDtypeStruct((b, s, MLA_HEADS * MLA_V), BF16),
        scratch_shapes=[
            pltpu.VMEM((1, tq), F32),
            pltpu.VMEM((MLA_VROWS, tq), F32),
            pltpu.VMEM((2, tk, tq), F32),
            pltpu.VMEM((2, 1, tq), F32),
        ],
        compiler_params=_params(("parallel", "parallel", "arbitrary")),
        name="mla_attn",
    )(qt, qt, kn, kr, vt)


def _swap_halves(w):
    half = w.shape[-1] // 2
    return jnp.concatenate([w[..., half:], w[..., :half]], axis=-1)


def _mla_layer(h, b, s, w_in, q_norm_g, kv_norm_g, w_uq, w_ukv):
    pad = LANES - MLA_ROPE
    w_cq = w_in[:, :MLA_Q_RANK]
    w_ckv = w_in[:, MLA_Q_RANK:MLA_Q_RANK + MLA_KV_RANK]
    w_kr = w_in[:, MLA_Q_RANK + MLA_KV_RANK:]
    pad_cols = lambda w: jnp.pad(w, [(0, 0)] * (w.ndim - 1) + [(0, pad)])
    w_uq = w_uq.reshape(MLA_Q_RANK, MLA_HEADS, MLA_QK)
    w_qt = jnp.pad(w_uq, [(0, 0), (0, 0), (0, 2 * LANES - MLA_QK)]).reshape(MLA_Q_RANK, -1).T
    w_qst = _swap_halves(w_uq[:, :, MLA_NOPE:]).reshape(MLA_Q_RANK, -1).T
    w_ukv = w_ukv.reshape(MLA_KV_RANK, MLA_HEADS, MLA_NOPE + MLA_V)
    w_kn = w_ukv[:, :, :MLA_NOPE].reshape(MLA_KV_RANK, MLA_HEADS * MLA_NOPE)
    w_vt = w_ukv[:, :, MLA_NOPE:].reshape(MLA_KV_RANK, MLA_HEADS * MLA_V).T
    inv = ROPE_BASE ** (-jnp.arange(0, MLA_ROPE, 2, dtype=F32) / MLA_ROPE)
    ang = jnp.arange(s, dtype=F32)[:, None] * inv[None, :]
    cos, sin = jnp.cos(ang), jnp.sin(ang)
    cos2 = jnp.concatenate([cos, cos], axis=1)
    sin2 = jnp.concatenate([-sin, sin], axis=1)
    tables = [jnp.concatenate([cos2, cos2], axis=1), jnp.concatenate([sin2, sin2], axis=1), cos2.T, sin2.T]
    bf = lambda w: w.astype(BF16)
    weights = [bf(w_cq), bf(w_ckv), bf(pad_cols(w_kr)), bf(pad_cols(_swap_halves(w_kr))),
               q_norm_g.reshape(1, -1), kv_norm_g.reshape(1, -1),
               bf(w_qt), bf(w_qst), bf(w_kn), bf(w_vt)]
    qt, kn, kr, vt = _mla_proj(h, b, s, tables, weights)
    sh = lambda t: t.reshape(b, s, t.shape[-1])
    o = _mla_attn(qt, sh(kn), sh(kr), vt)
    return o.reshape(b * s, MLA_HEADS * MLA_V)


def kernel(x, gla_w_in, gla_w_gate_up, gla_b_gate, gla_norm_g, gla_w_out,
           rpa_w_in, rpa_rel_bias, rpa_w_out,
           mla_w_in, mla_q_norm_g, mla_kv_norm_g, mla_w_uq, mla_w_ukv, mla_w_out,
           ffn_w_in, ffn_w_out, ln_g, ln_b):
    b, s, d = x.shape
    h = x.reshape(b * s, d)
    assert ffn_w_out.shape[1] % FFN_CHUNK == 0
    for i in range(DEPTH):
        m, j = i % N_MIXERS, i // N_MIXERS
        if m == 0:
            o = _gla_layer(h, b, s, gla_w_in[j], gla_w_gate_up[j], gla_b_gate[j], gla_norm_g[j])
            w_o = gla_w_out[j]
        elif m == 1:
            o = _rpa_layer(h, b, s, rpa_w_in[j], rpa_rel_bias[j])
            w_o = rpa_w_out[j]
        else:
            o = _mla_layer(h, b, s, mla_w_in[j], mla_q_norm_g[j], mla_kv_norm_g[j], mla_w_uq[j], mla_w_ukv[j])
            w_o = mla_w_out[j]
        ln = jnp.stack([ln_g[i, 0], ln_b[i, 0], ln_g[i, 1], ln_b[i, 1]])
        h = _post(o, h, w_o.astype(BF16), ln, ffn_w_in[i].astype(BF16), ffn_w_out[i].astype(BF16))
    return h.reshape(b, s, d)
```

```python
import functools

import jax
import jax.numpy as jnp
import numpy as np
from jax import lax
from jax.experimental import pallas as pl
from jax.experimental.pallas import tpu as pltpu

F32 = jnp.float32
BF16 = jnp.bfloat16

DEPTH = 4
N_MIXERS = 3
CHUNK = 64
DEEPNORM_ALPHA = (2 * DEPTH) ** 0.25
LN_EPS = 1e-5
RMS_EPS = 1e-6
NEG_INF = -1e30

GLA_HEADS = 4
GLA_DK = 128
GLA_DV = 256
GLA_GATE_RANK = 16
GLA_TAU = 16.0
GLA_HK = GLA_HEADS * GLA_DK
GLA_HV = GLA_HEADS * GLA_DV
GLA_BLOCK = 512

RPA_HEADS = 16
RPA_HEAD_DIM = 64
RPA_LEFT_CHUNKS = 8
RPA_MAX_REL = 128
RPA_GROUP = 2
RPA_GQ = RPA_GROUP * CHUNK
RPA_GK = (RPA_LEFT_CHUNKS + RPA_GROUP) * CHUNK
RPA_TQ = 512
RPA_PAIRS = RPA_HEADS // 2

MLA_HEADS = 8
MLA_NOPE = 128
MLA_ROPE = 64
MLA_V = 128
MLA_Q_RANK = 384
MLA_KV_RANK = 256
MLA_QK = MLA_NOPE + MLA_ROPE
ROPE_BASE = 10000.0
MLA_TQ = 1024
MLA_TK = 512
BF16_SUBLANES = 16
MLA_VROWS = MLA_V + BF16_SUBLANES
LOG2E = 1.4426950408889634
RPA_VROWS = 128 + BF16_SUBLANES

LANES = 128
VMEM_LIMIT = 56 * 1024 * 1024

TOKEN_TILE = 512
POST_SUBTILES = 2
FFN_CHUNK = 256


def _params(sem):
    return pltpu.CompilerParams(dimension_semantics=sem, vmem_limit_bytes=VMEM_LIMIT)


def _const_spec(shape):
    nd = len(shape)
    return pl.BlockSpec(shape, lambda *_: (0,) * nd, pipeline_mode=pl.Buffered(1))


def _layer_norm(x, g, b):
    mu = jnp.mean(x, axis=-1, keepdims=True)
    xc = x - mu
    var = jnp.mean(xc * xc, axis=-1, keepdims=True)
    return xc * lax.rsqrt(var + LN_EPS) * g + b


def _dot(a, b):
    return jnp.dot(a, b, preferred_element_type=F32)


def _dot_nt(a, b):
    return lax.dot_general(a, b, (((1,), (1,)), ((), ())), preferred_element_type=F32)


def _dot_tn(a, b):
    return lax.dot_general(a, b, (((0,), (0,)), ((), ())), preferred_element_type=F32)


def _post_kernel(o_ref, h_ref, wo_ref, ln_ref, wgu_ref, wdn_ref, out_ref, xb_ref):
    tm = out_ref.shape[0]
    hidden = wdn_ref.shape[0]
    halves = [slice(r0, r0 + TOKEN_TILE) for r0 in range(0, tm, TOKEN_TILE)]

    def mixer_out_norm(rows):
        y = _dot(o_ref[rows, :], wo_ref[...])
        h1 = _layer_norm(DEEPNORM_ALPHA * h_ref[rows, :] + y, ln_ref[0:1, :], ln_ref[1:2, :])
        out_ref[rows, :] = h1
        xb_ref[rows, :] = h1.astype(BF16)

    def ffn_norm(rows):
        xb = xb_ref[rows, :]
        acc = None
        for c0 in range(0, hidden, FFN_CHUNK):
            g = _dot(xb, wgu_ref[:, c0:c0 + FFN_CHUNK])
            u = _dot(xb, wgu_ref[:, hidden + c0:hidden + c0 + FFN_CHUNK])
            a = (g * jax.nn.sigmoid(g) * u).astype(BF16)
            part = _dot(a, wdn_ref[c0:c0 + FFN_CHUNK, :])
            acc = part if acc is None else acc + part
        out_ref[rows, :] = _layer_norm(DEEPNORM_ALPHA * out_ref[rows, :] + acc, ln_ref[2:3, :], ln_ref[3:4, :])

    for rows in halves:
        mixer_out_norm(rows)
    for rows in halves:
        ffn_norm(rows)


def _post(o, h, w_o, ln, w_gu, w_dn):
    m, d = h.shape
    tm = POST_SUBTILES * TOKEN_TILE
    row = lambda i: (i, 0)
    return pl.pallas_call(
        _post_kernel,
        grid=(m // tm,),
        in_specs=[
            pl.BlockSpec((tm, o.shape[1]), row),
            pl.BlockSpec((tm, d), row),
            _const_spec(w_o.shape),
            _const_spec(ln.shape),
            _const_spec(w_gu.shape),
            _const_spec(w_dn.shape),
        ],
        out_specs=pl.BlockSpec((tm, d), row),
        out_shape=jax.ShapeDtypeStruct((m, d), F32),
        scratch_shapes=[pltpu.VMEM((tm, d), BF16)],
        compiler_params=_params(("parallel",)),
        name="post_ffn",
    )(o, h, w_o, ln, w_gu, w_dn)


def _gla_proj_kernel(h_ref, wm_ref, wgl_ref, wgu_ref, bg_ref, qk_ref, v_ref, r_ref, la_ref):
    xb = h_ref[...].astype(BF16)
    g_low = _dot(xb, wgl_ref[...])
    z = _dot(g_low.astype(BF16), wgu_ref[...]) + bg_ref[...]
    log_sig = jnp.minimum(z, 0.0) - jnp.log(1.0 + jnp.exp(-jnp.abs(z)))
    la_ref[...] = log_sig / GLA_TAU
    qk_ref[...] = _dot(xb, wm_ref[:, 0:2 * GLA_HK])
    v_ref[...] = _dot(xb, wm_ref[:, 2 * GLA_HK:2 * GLA_HK + GLA_HV]).astype(BF16)
    r_ref[...] = _dot(xb, wm_ref[:, 2 * GLA_HK + GLA_HV:])


def _gla_proj(h, w_main, w_gl, w_gu, b_gate):
    m, d = h.shape
    tm = TOKEN_TILE
    row = lambda i: (i, 0)
    return pl.pallas_call(
        _gla_proj_kernel,
        grid=(m // tm,),
        in_specs=[
            pl.BlockSpec((tm, d), row),
            _const_spec(w_main.shape),
            _const_spec(w_gl.shape),
            _const_spec(w_gu.shape),
            _const_spec(b_gate.shape),
        ],
        out_specs=[
            pl.BlockSpec((tm, 2 * GLA_HK), row),
            pl.BlockSpec((tm, GLA_HV), row),
            pl.BlockSpec((tm, GLA_HV), row),
            pl.BlockSpec((tm, GLA_HK), row),
        ],
        out_shape=[
            jax.ShapeDtypeStruct((m, 2 * GLA_HK), F32),
            jax.ShapeDtypeStruct((m, GLA_HV), BF16),
            jax.ShapeDtypeStruct((m, GLA_HV), F32),
            jax.ShapeDtypeStruct((m, GLA_HK), F32),
        ],
        compiler_params=_params(("parallel",)),
        name="gla_proj",
    )(h, w_main, w_gl, w_gu, b_gate)


def _split3_bf16(x):
    hi = x.astype(BF16)
    r1 = x - hi.astype(F32)
    mid = r1.astype(BF16)
    lo = (r1 - mid.astype(F32)).astype(BF16)
    return hi, mid, lo


def _gla_core_kernel(qk_ref, v_ref, r_ref, la_ref, g_ref, y_ref, state_ref, *, n_chunks):
    @pl.when(pl.program_id(1) == 0)
    def _():
        state_ref[...] = jnp.zeros_like(state_ref)

    row_i = lax.broadcasted_iota(jnp.int32, (CHUNK, CHUNK), 0)
    col_i = lax.broadcasted_iota(jnp.int32, (CHUNK, CHUNK), 1)
    causal = col_i <= row_i
    tri = causal.astype(BF16)

    def prepare(ci):
        rows = slice(ci * CHUNK, (ci + 1) * CHUNK)
        la = la_ref[0, rows, :]
        hi, mid, lo = _split3_bf16(la)
        cum = _dot(tri, hi) + _dot(tri, mid) + _dot(tri, lo)
        tot = cum[CHUNK - 1:CHUNK, :]
        q = qk_ref[0, rows, 0:GLA_HK] * (GLA_DK ** -0.5)
        k = qk_ref[0, rows, GLA_HK:2 * GLA_HK]
        q_dec = (q * jnp.exp(cum)).astype(BF16)
        k_inv = (k * jnp.exp(-cum)).astype(BF16)
        k_end = (k * jnp.exp(tot - cum)).astype(BF16)
        decay = jnp.exp(tot)
        o_intra = []
        for h in range(GLA_HEADS):
            kc = slice(h * GLA_DK, (h + 1) * GLA_DK)
            vc = slice(h * GLA_DV, (h + 1) * GLA_DV)
            a = jnp.where(causal, _dot_nt(q_dec[:, kc], k_inv[:, kc]), 0.0)
            o_intra.append(_dot(a.astype(BF16), v_ref[0, rows, vc]))
        return q_dec, k_end, decay, o_intra

    def recur(ci, prepared):
        rows = slice(ci * CHUNK, (ci + 1) * CHUNK)
        q_dec, k_end, decay, o_intra = prepared
        for h in range(GLA_HEADS):
            kc = slice(h * GLA_DK, (h + 1) * GLA_DK)
            vc = slice(h * GLA_DV, (h + 1) * GLA_DV)
            vh = v_ref[0, rows, vc]
            st = state_ref[h]
            o = o_intra[h] + _dot_nt(q_dec[:, kc], st.astype(BF16))
            state_ref[h] = st * decay[:, kc] + _dot_tn(vh, k_end[:, kc])
            mu = jnp.mean(o, axis=-1, keepdims=True)
            oc = o - mu
            var = jnp.mean(oc * oc, axis=-1, keepdims=True)
            on = oc * lax.rsqrt(var + LN_EPS) * g_ref[:, vc]
            r = r_ref[0, rows, vc]
            y_ref[0, rows, vc] = (r * jax.nn.sigmoid(r) * on).astype(BF16)

    prepared = prepare(0)
    for ci in range(n_chunks):
        upcoming = prepare(ci + 1) if ci + 1 < n_chunks else None
        recur(ci, prepared)
        prepared = upcoming


def _gla_core(qk, v, r, la, norm_g, *, block):
    b, s, _ = qk.shape
    blk = lambda bi, ti: (bi, ti, 0)
    return pl.pallas_call(
        functools.partial(_gla_core_kernel, n_chunks=block // CHUNK),
        grid=(b, s // block),
        in_specs=[
            pl.BlockSpec((1, block, 2 * GLA_HK), blk),
            pl.BlockSpec((1, block, GLA_HV), blk),
            pl.BlockSpec((1, block, GLA_HV), blk),
            pl.BlockSpec((1, block, GLA_HK), blk),
            pl.BlockSpec((1, GLA_HV), lambda bi, ti: (0, 0)),
        ],
        out_specs=pl.BlockSpec((1, block, GLA_HV), blk),
        out_shape=jax.ShapeDtypeStruct((b, s, GLA_HV), BF16),
        scratch_shapes=[pltpu.VMEM((GLA_HEADS, GLA_DV, GLA_DK), F32)],
        compiler_params=_params(("parallel", "arbitrary")),
        name="gla_core",
    )(qk, v, r, la, norm_g)


def _gla_layer(h, b, s, w_in, w_gate_up, b_gate, norm_g):
    d = h.shape[1]
    n_main = 2 * GLA_HK + 2 * GLA_HV
    w_main = w_in[:, :n_main].astype(BF16)
    w_gl = jnp.zeros((d, LANES), BF16).at[:, :GLA_GATE_RANK].set(w_in[:, n_main:].astype(BF16))
    w_gu = jnp.zeros((LANES, GLA_HK), BF16).at[:GLA_GATE_RANK, :].set(w_gate_up.astype(BF16))
    qk, v, r, la = _gla_proj(h, w_main, w_gl, w_gu, b_gate.reshape(1, GLA_HK))
    sh = lambda t: t.reshape(b, s, t.shape[-1])
    y = _gla_core(sh(qk), sh(v), sh(r), sh(la), norm_g.reshape(1, GLA_HV), block=GLA_BLOCK)
    return y.reshape(b * s, GLA_HV)


def _rpa_proj_kernel(h_ref, wqt_ref, wk_ref, wvt_ref, qt_ref, k_ref, vt_ref):
    xb = h_ref[...].astype(BF16)
    tm = h_ref.shape[0]
    qt_ref[0] = (_dot_nt(wqt_ref[...], xb) * (RPA_HEAD_DIM ** -0.5 * LOG2E)).astype(BF16)
    k_ref[...] = _dot(xb, wk_ref[...]).astype(BF16)
    vt = _dot_nt(wvt_ref[...], xb).astype(BF16)
    for p in range(RPA_PAIRS):
        r0 = p * RPA_VROWS
        vt_ref[0, r0:r0 + LANES, :] = vt[p * LANES:(p + 1) * LANES]
        vt_ref[0, r0 + LANES:r0 + RPA_VROWS, :] = jnp.ones((BF16_SUBLANES, tm), BF16)


def _rpa_proj(h, b, s, w_qt, w_k, w_vt):
    m, d = h.shape
    tm = RPA_TQ
    nt = s // tm
    row = lambda i: (i, 0)
    col = lambda i: (i // nt, 0, i % nt)
    return pl.pallas_call(
        _rpa_proj_kernel,
        grid=(m // tm,),
        in_specs=[pl.BlockSpec((tm, d), row), _const_spec(w_qt.shape), _const_spec(w_k.shape),
                  _const_spec(w_vt.shape)],
        out_specs=[pl.BlockSpec((1, d, tm), col), pl.BlockSpec((tm, d), row),
                   pl.BlockSpec((1, RPA_PAIRS * RPA_VROWS, tm), col)],
        out_shape=[jax.ShapeDtypeStruct((b, d, s), BF16), jax.ShapeDtypeStruct((m, d), BF16),
                   jax.ShapeDtypeStruct((b, RPA_PAIRS * RPA_VROWS, s), BF16)],
        compiler_params=_params(("parallel",)),
        name="rpa_proj",
    )(h, w_qt, w_k, w_vt)


def _rpa_core_kernel(qt_ref, kp_ref, kc_ref, vtp_ref, vtc_ref, bias_ref, o_ref, kw_ref, vw_ref, st_ref, mx_ref,
                     e_ref):
    t = pl.program_id(1)
    tq = o_ref.shape[1]
    kw_ref[0:tq, :] = kp_ref[0]
    kw_ref[tq:2 * tq, :] = kc_ref[0]
    vw_ref[:, 0:tq] = vtp_ref[0]
    vw_ref[:, tq:2 * tq] = vtc_ref[0]

    first_head = lax.broadcasted_iota(jnp.int32, (LANES, RPA_GQ), 0) < RPA_HEAD_DIM
    key_row = lax.broadcasted_iota(jnp.int32, (RPA_GK, 2 * RPA_GQ), 0)

    def scores(slot, g, p):
        q0 = g * RPA_GQ
        feat = slice(p * LANES, (p + 1) * LANES)
        qt = qt_ref[0, feat, q0:q0 + RPA_GQ]
        zero = jnp.zeros_like(qt)
        q_blk = jnp.concatenate([jnp.where(first_head, qt, zero), jnp.where(first_head, zero, qt)], axis=1)
        st = _dot(kw_ref[q0:q0 + RPA_GK, feat], q_blk) + bias_ref[p]
        st = jnp.where(jnp.logical_and(t == 0, key_row + q0 < tq), NEG_INF, st)
        st_ref[slot] = st
        mx_ref[slot] = jnp.max(st, axis=0, keepdims=True)

    def softmax(slot, g, p):
        e_ref[slot] = jnp.exp2(st_ref[slot] - mx_ref[slot]).astype(BF16)

    def values(slot, g, p):
        q0 = g * RPA_GQ
        feat = slice(p * LANES, (p + 1) * LANES)
        ot = _dot(vw_ref[p * RPA_VROWS:(p + 1) * RPA_VROWS, q0:q0 + RPA_GK], e_ref[slot])
        on = ot[0:LANES, :] * (1.0 / ot[LANES:LANES + 1, :])
        pair_t = jnp.where(first_head, on[:, 0:RPA_GQ], on[:, RPA_GQ:2 * RPA_GQ])
        o_ref[0, q0:q0 + RPA_GQ, feat] = pair_t.T.astype(BF16)

    units = [(g, p) for g in range(tq // RPA_GQ) for p in range(RPA_PAIRS)]
    n = len(units)
    scores(0, *units[0])
    scores(1, *units[1])
    softmax(0, *units[0])
    for i in range(n):
        if i + 2 < n:
            scores(i % 2, *units[i + 2])
        if i + 1 < n:
            softmax((i + 1) % 2, *units[i + 1])
        values(i % 2, *units[i])


def _rpa_core(qt, k, vt, bias_t):
    b, s, d = k.shape
    tq = RPA_TQ
    vrows = vt.shape[1]
    prev = lambda ti: jnp.maximum(ti - 1, 0)
    return pl.pallas_call(
        _rpa_core_kernel,
        grid=(b, s // tq),
        in_specs=[
            pl.BlockSpec((1, d, tq), lambda bi, ti: (bi, 0, ti)),
            pl.BlockSpec((1, tq, d), lambda bi, ti: (bi, prev(ti), 0)),
            pl.BlockSpec((1, tq, d), lambda bi, ti: (bi, ti, 0)),
            pl.BlockSpec((1, vrows, tq), lambda bi, ti: (bi, 0, prev(ti))),
            pl.BlockSpec((1, vrows, tq), lambda bi, ti: (bi, 0, ti)),
            _const_spec(bias_t.shape),
        ],
        out_specs=pl.BlockSpec((1, tq, d), lambda bi, ti: (bi, ti, 0)),
        out_shape=jax.ShapeDtypeStruct((b, s, d), BF16),
        scratch_shapes=[pltpu.VMEM((2 * tq, d), BF16), pltpu.VMEM((vrows, 2 * tq), BF16),
                        pltpu.VMEM((2, RPA_GK, 2 * RPA_GQ), F32), pltpu.VMEM((2, 1, 2 * RPA_GQ), F32),
                        pltpu.VMEM((2, RPA_GK, 2 * RPA_GQ), BF16)],
        compiler_params=_params(("parallel", "parallel")),
        name="rpa_core",
    )(qt, k, k, vt, vt, bias_t)


def _rpa_group_bias_t(rel_bias):
    gq, gk = RPA_GQ, RPA_GK
    span = gq + gk - 1
    diag = np.arange(span)
    rel = np.clip(RPA_LEFT_CHUNKS * CHUNK + gq - 1 - diag, -RPA_MAX_REL, RPA_MAX_REL) + RPA_MAX_REL
    u = jnp.pad(rel_bias[:, rel].astype(F32), ((0, 0), (0, 1)))
    shifted = jnp.tile(u, (1, gq))[:, :gq * span].reshape(-1, gq, span)
    table = shifted[:, :, gq - 1:gq - 1 + gk]
    r = np.arange(gq)[:, None]
    w = np.arange(gk)[None, :]
    band = w - (r // CHUNK) * CHUNK
    in_band = (band >= 0) & (band < (RPA_LEFT_CHUNKS + 1) * CHUNK)
    table = jnp.where(in_band[None], table * LOG2E, NEG_INF)
    table_t = table.transpose(0, 2, 1).reshape(RPA_PAIRS, 2, gk, gq)
    return table_t.transpose(0, 2, 1, 3).reshape(RPA_PAIRS, gk, 2 * gq)


def _rpa_layer(h, b, s, w_in, rel_bias):
    d = h.shape[1]
    w = w_in.astype(BF16)
    qt, k, vt = _rpa_proj(h, b, s, w[:, 0:d].T, w[:, d:2 * d], w[:, 2 * d:3 * d].T)
    o = _rpa_core(qt, k.reshape(b, s, d), vt, _rpa_group_bias_t(rel_bias))
    return o.reshape(b * s, d)


def _rms_norm(x, g):
    return x * lax.rsqrt(jnp.mean(x * x, axis=-1, keepdims=True) + RMS_EPS) * g


def _mla_proj_kernel(h_ref, cos_ref, sin_ref, cost_ref, sint_ref, wcq_ref, wckv_ref, wkr_ref,
                     gq_ref, gkv_ref, wqt_ref, wqst_ref, wkn_ref, wvt_ref,
                     qt_ref, kn_ref, kr_ref, vt_ref):
    xb = h_ref[...].astype(BF16)
    tm = h_ref.shape[0]
    scale = MLA_QK ** -0.5 * LOG2E
    cq = _rms_norm(_dot(xb, wcq_ref[...]), gq_ref[...]).astype(BF16)
    ckv = _rms_norm(_dot(xb, wckv_ref[...]), gkv_ref[...]).astype(BF16)
    kn_ref[...] = _dot(ckv, wkn_ref[...]).astype(BF16)
    kr2 = _dot(xb, wkr_ref[...])
    kr = kr2[:, 0:LANES] * cos_ref[...] + kr2[:, LANES:2 * LANES] * sin_ref[...]
    kr_ref[...] = kr.astype(BF16)
    vt = _dot_nt(wvt_ref[...], ckv).astype(BF16)
    for h in range(MLA_HEADS):
        r0 = h * MLA_VROWS
        vt_ref[0, 0, r0:r0 + MLA_V, :] = vt[h * MLA_V:(h + 1) * MLA_V]
        vt_ref[0, 0, r0 + MLA_V:r0 + MLA_VROWS, :] = jnp.ones((BF16_SUBLANES, tm), BF16)
    a = _dot_nt(wqt_ref[...], cq)
    bsw = _dot_nt(wqst_ref[...], cq)
    cos_t = cost_ref[...]
    sin_t = sint_ref[...]
    hq = 2 * LANES
    for h in range(MLA_HEADS):
        r0 = h * hq
        qt_ref[0, 0, r0:r0 + MLA_NOPE, :] = (a[r0:r0 + MLA_NOPE] * scale).astype(BF16)
        rope = a[r0 + MLA_NOPE:r0 + MLA_QK] * cos_t + bsw[h * MLA_ROPE:(h + 1) * MLA_ROPE] * sin_t
        qt_ref[0, 0, r0 + MLA_NOPE:r0 + MLA_QK, :] = (rope * scale).astype(BF16)
        qt_ref[0, 0, r0 + MLA_QK:r0 + hq, :] = jnp.zeros((hq - MLA_QK, qt_ref.shape[3]), BF16)


def _mla_proj(h, b, s, tables, weights):
    m, d = h.shape
    tm = MLA_TK
    nt = s // tm
    row = lambda i: (i, 0)
    pos = lambda i: (i % nt, 0)
    pos_t = lambda i: (0, i % nt)
    tile_t = lambda i: (i // nt, i % nt, 0, 0)
    wide = MLA_HEADS * LANES
    return pl.pallas_call(
        _mla_proj_kernel,
        grid=(m // tm,),
        in_specs=[pl.BlockSpec((tm, d), row), pl.BlockSpec((tm, LANES), pos), pl.BlockSpec((tm, LANES), pos),
                  pl.BlockSpec((MLA_ROPE, tm), pos_t), pl.BlockSpec((MLA_ROPE, tm), pos_t)]
        + [_const_spec(w.shape) for w in weights],
        out_specs=[pl.BlockSpec((1, 1, 2 * wide, tm), tile_t), pl.BlockSpec((tm, wide), row),
                   pl.BlockSpec((tm, LANES), row), pl.BlockSpec((1, 1, MLA_HEADS * MLA_VROWS, tm), tile_t)],
        out_shape=[jax.ShapeDtypeStruct((b, nt, 2 * wide, tm), BF16), jax.ShapeDtypeStruct((m, wide), BF16),
                   jax.ShapeDtypeStruct((m, LANES), BF16),
                   jax.ShapeDtypeStruct((b, nt, MLA_HEADS * MLA_VROWS, tm), BF16)],
        compiler_params=_params(("parallel",)),
        name="mla_proj",
    )(h, *tables, *weights)


def _mla_attn_kernel(qt_ref, qnext_ref, kn_ref, kr_ref, vt_ref, o_ref, m_ref, acc_ref, st_ref, mx_ref):
    qi = pl.program_id(2)
    tq, tk = MLA_TQ, MLA_TK
    assert tq == 2 * tk
    m_ref[...] = jnp.full_like(m_ref, NEG_INF)
    acc_ref[...] = jnp.zeros_like(acc_ref)

    def scores(slot, j, q_ref=qt_ref):
        rows = pl.ds(pl.multiple_of(j * tk, tk), tk)
        kcat = jnp.concatenate([kn_ref[0, rows, :], kr_ref[0, rows, :]], axis=1)
        st_ref[slot, :, 0:tk] = _dot(kcat, q_ref[0, 0])
        st_ref[slot, :, tk:tq] = _dot(kcat, q_ref[0, 1])
        mx_ref[slot] = jnp.max(st_ref[slot], axis=0, keepdims=True)

    def consume(slot, j, first_key_chunk=None):
        st = st_ref[slot]
        if first_key_chunk is None:
            mx = mx_ref[slot]
        else:
            kchunk = first_key_chunk + lax.broadcasted_iota(jnp.int32, (tk, tq), 0) // CHUNK
            qchunk = lax.broadcasted_iota(jnp.int32, (tk, tq), 1) // CHUNK
            st = jnp.where(kchunk <= qchunk, st, NEG_INF)
            mx = jnp.max(st, axis=0, keepdims=True)
        m_prev = m_ref[...]
        m_new = jnp.maximum(m_prev, mx)
        alpha = jnp.exp2(m_prev - m_new)
        p = jnp.exp2(st - m_new).astype(BF16)
        acc_ref[...] = alpha * acc_ref[...] + _dot(vt_ref[0, j], p)
        m_ref[...] = m_new

    @pl.when(qi == 0)
    def _():
        scores(0, 0)

    def pair(i):
        j = 2 * i
        scores(1, j + 1)
        consume(0, j)
        scores(0, j + 2)
        consume(1, j + 1)

    def two_pairs(i, carry):
        pair(2 * i)
        pair(2 * i + 1)
        return carry

    lax.fori_loop(0, qi // 2, two_pairs, 0)

    @pl.when(qi % 2 == 1)
    def _():
        pair(qi - 1)

    scores(1, 2 * qi + 1)
    consume(0, 2 * qi, first_key_chunk=0)
    scores(0, 0, qnext_ref)
    consume(1, 2 * qi + 1, first_key_chunk=tk // CHUNK)

    acc = acc_ref[...]
    o_ref[0] = (acc[0:MLA_V] * (1.0 / acc[MLA_V:MLA_V + 1])).T.astype(BF16)


def _mla_attn(qt, kn, kr, vt):
    b, nt, _, tk = qt.shape
    s = kn.shape[1]
    tq = MLA_TQ
    assert MLA_TK == tk
    last_q = s // tq - 1
    return pl.pallas_call(
        _mla_attn_kernel,
        grid=(b, MLA_HEADS, s // tq),
        in_specs=[
            pl.BlockSpec((1, tq // tk, 2 * LANES, tk), lambda bi, hi, qi: (bi, qi, hi, 0)),
            pl.BlockSpec((1, tq // tk, 2 * LANES, tk), lambda bi, hi, qi: (bi, jnp.minimum(qi + 1, last_q), hi, 0)),
            pl.BlockSpec((1, s, LANES), lambda bi, hi, qi: (bi, 0, hi)),
            pl.BlockSpec((1, s, LANES), lambda bi, hi, qi: (bi, 0, 0)),
            pl.BlockSpec((1, nt, MLA_VROWS, tk), lambda bi, hi, qi: (bi, 0, hi, 0)),
        ],
        out_specs=pl.BlockSpec((1, tq, LANES), lambda bi, hi, qi: (bi, qi, hi)),
        out_shape=jax.ShapeDtypeStruct((b, s, MLA_HEADS * MLA_V), BF16),
        scratch_shapes=[
            pltpu.VMEM((1, tq), F32),
            pltpu.VMEM((MLA_VROWS, tq), F32),
            pltpu.VMEM((2, tk, tq), F32),
            pltpu.VMEM((2, 1, tq), F32),
        ],
        compiler_params=_params(("parallel", "parallel", "arbitrary")),
        name="mla_attn",
    )(qt, qt, kn, kr, vt)


def _swap_halves(w):
    half = w.shape[-1] // 2
    return jnp.concatenate([w[..., half:], w[..., :half]], axis=-1)


def _mla_layer(h, b, s, w_in, q_norm_g, kv_norm_g, w_uq, w_ukv):
    pad = LANES - MLA_ROPE
    w_cq = w_in[:, :MLA_Q_RANK]
    w_ckv = w_in[:, MLA_Q_RANK:MLA_Q_RANK + MLA_KV_RANK]
    w_kr = w_in[:, MLA_Q_RANK + MLA_KV_RANK:]
    pad_cols = lambda w: jnp.pad(w, [(0, 0)] * (w.ndim - 1) + [(0, pad)])
    w_uq = w_uq.reshape(MLA_Q_RANK, MLA_HEADS, MLA_QK)
    w_qt = jnp.pad(w_uq, [(0, 0), (0, 0), (0, 2 * LANES - MLA_QK)]).reshape(MLA_Q_RANK, -1).T
    w_qst = _swap_halves(w_uq[:, :, MLA_NOPE:]).reshape(MLA_Q_RANK, -1).T
    w_ukv = w_ukv.reshape(MLA_KV_RANK, MLA_HEADS, MLA_NOPE + MLA_V)
    w_kn = w_ukv[:, :, :MLA_NOPE].reshape(MLA_KV_RANK, MLA_HEADS * MLA_NOPE)
    w_vt = w_ukv[:, :, MLA_NOPE:].reshape(MLA_KV_RANK, MLA_HEADS * MLA_V).T
    inv = ROPE_BASE ** (-jnp.arange(0, MLA_ROPE, 2, dtype=F32) / MLA_ROPE)
    ang = jnp.arange(s, dtype=F32)[:, None] * inv[None, :]
    cos, sin = jnp.cos(ang), jnp.sin(ang)
    cos2 = jnp.concatenate([cos, cos], axis=1)
    sin2 = jnp.concatenate([-sin, sin], axis=1)
    tables = [jnp.concatenate([cos2, cos2], axis=1), jnp.concatenate([sin2, sin2], axis=1), cos2.T, sin2.T]
    bf = lambda w: w.astype(BF16)
    w_kr2 = jnp.concatenate([pad_cols(w_kr), pad_cols(_swap_halves(w_kr))], axis=1)
    weights = [bf(w_cq), bf(w_ckv), bf(w_kr2),
               q_norm_g.reshape(1, -1), kv_norm_g.reshape(1, -1),
               bf(w_qt), bf(w_qst), bf(w_kn), bf(w_vt)]
    qt, kn, kr, vt = _mla_proj(h, b, s, tables, weights)
    sh = lambda t: t.reshape(b, s, t.shape[-1])
    o = _mla_attn(qt, sh(kn), sh(kr), vt)
    return o.reshape(b * s, MLA_HEADS * MLA_V)


def kernel(x, gla_w_in, gla_w_gate_up, gla_b_gate, gla_norm_g, gla_w_out,
           rpa_w_in, rpa_rel_bias, rpa_w_out,
           mla_w_in, mla_q_norm_g, mla_kv_norm_g, mla_w_uq, mla_w_ukv, mla_w_out,
           ffn_w_in, ffn_w_out, ln_g, ln_b):
    b, s, d = x.shape
    h = x.reshape(b * s, d)
    assert ffn_w_out.shape[1] % FFN_CHUNK == 0
    for i in range(DEPTH):
        m, j = i % N_MIXERS, i // N_MIXERS
        if m == 0:
            o = _gla_layer(h, b, s, gla_w_in[j], gla_w_gate_up[j], gla_b_gate[j], gla_norm_g[j])
            w_o = gla_w_out[j]
        elif m == 1:
            o = _rpa_layer(h, b, s, rpa_w_in[j], rpa_rel_bias[j])
            w_o = rpa_w_out[j]
        else:
            o = _mla_layer(h, b, s, mla_w_in[j], mla_q_norm_g[j], mla_kv_norm_g[j], mla_w_uq[j], mla_w_ukv[j])
            w_o = mla_w_out[j]
        ln = jnp.stack([ln_g[i, 0], ln_b[i, 0], ln_g[i, 1], ln_b[i, 1]])
        h = _post(o, h, w_o.astype(BF16), ln, ffn_w_in[i].astype(BF16), ffn_w_out[i].astype(BF16))
    return h.reshape(b, s, d)
```

```python
import functools

import jax
import jax.numpy as jnp
import numpy as np
from jax import lax
from jax.experimental import pallas as pl
from jax.experimental.pallas import tpu as pltpu

F32 = jnp.float32
BF16 = jnp.bfloat16

DEPTH = 4
N_MIXERS = 3
CHUNK = 64
DEEPNORM_ALPHA = (2 * DEPTH) ** 0.25
LN_EPS = 1e-5
RMS_EPS = 1e-6
NEG_INF = -1e30

GLA_HEADS = 4
GLA_DK = 128
GLA_DV = 256
GLA_GATE_RANK = 16
GLA_TAU = 16.0
GLA_HK = GLA_HEADS * GLA_DK
GLA_HV = GLA_HEADS * GLA_DV
GLA_BLOCK = 1024

RPA_HEADS = 16
RPA_HEAD_DIM = 64
RPA_LEFT_CHUNKS = 8
RPA_MAX_REL = 128
RPA_GROUP = 2
RPA_GQ = RPA_GROUP * CHUNK
RPA_GK = (RPA_LEFT_CHUNKS + RPA_GROUP) * CHUNK
RPA_TQ = 512
RPA_PAIRS = RPA_HEADS // 2

MLA_HEADS = 8
MLA_NOPE = 128
MLA_ROPE = 64
MLA_V = 128
MLA_Q_RANK = 384
MLA_KV_RANK = 256
MLA_QK = MLA_NOPE + MLA_ROPE
ROPE_BASE = 10000.0
MLA_TQ = 1024
MLA_TK = 512
BF16_SUBLANES = 16
MLA_VROWS = MLA_V + BF16_SUBLANES
LOG2E = 1.4426950408889634
MLA_ST_PAD = 128
RPA_VROWS = 128 + BF16_SUBLANES

LANES = 128
VMEM_LIMIT = 56 * 1024 * 1024

TOKEN_TILE = 512
POST_SUBTILES = 2
FFN_CHUNK = 256


def _params(sem):
    return pltpu.CompilerParams(dimension_semantics=sem, vmem_limit_bytes=VMEM_LIMIT)


def _const_spec(shape):
    nd = len(shape)
    return pl.BlockSpec(shape, lambda *_: (0,) * nd, pipeline_mode=pl.Buffered(1))


def _layer_norm(x, g, b):
    mu = jnp.mean(x, axis=-1, keepdims=True)
    xc = x - mu
    var = jnp.mean(xc * xc, axis=-1, keepdims=True)
    return xc * lax.rsqrt(var + LN_EPS) * g + b


def _dot(a, b):
    return jnp.dot(a, b, preferred_element_type=F32)


def _dot_nt(a, b):
    return lax.dot_general(a, b, (((1,), (1,)), ((), ())), preferred_element_type=F32)


def _dot_tn(a, b):
    return lax.dot_general(a, b, (((0,), (0,)), ((), ())), preferred_element_type=F32)


def _post_kernel(o_ref, h_ref, wo_ref, ln_ref, wgu_ref, wdn_ref, out_ref, xb_ref):
    tm = out_ref.shape[0]
    hidden = wdn_ref.shape[0]
    halves = [slice(r0, r0 + TOKEN_TILE) for r0 in range(0, tm, TOKEN_TILE)]

    def mixer_out_norm(rows):
        y = _dot(o_ref[rows, :], wo_ref[...])
        h1 = _layer_norm(DEEPNORM_ALPHA * h_ref[rows, :] + y, ln_ref[0:1, :], ln_ref[1:2, :])
        out_ref[rows, :] = h1
        xb_ref[rows, :] = h1.astype(BF16)

    def ffn_norm(rows):
        xb = xb_ref[rows, :]
        acc = None
        for c0 in range(0, hidden, FFN_CHUNK):
            g = _dot(xb, wgu_ref[:, c0:c0 + FFN_CHUNK])
            u = _dot(xb, wgu_ref[:, hidden + c0:hidden + c0 + FFN_CHUNK])
            a = (g * jax.nn.sigmoid(g) * u).astype(BF16)
            part = _dot(a, wdn_ref[c0:c0 + FFN_CHUNK, :])
            acc = part if acc is None else acc + part
        out_ref[rows, :] = _layer_norm(DEEPNORM_ALPHA * out_ref[rows, :] + acc, ln_ref[2:3, :], ln_ref[3:4, :])

    for rows in halves:
        mixer_out_norm(rows)
    for rows in halves:
        ffn_norm(rows)


def _post(o, h, w_o, ln, w_gu, w_dn):
    m, d = h.shape
    tm = POST_SUBTILES * TOKEN_TILE
    row = lambda i: (i, 0)
    return pl.pallas_call(
        _post_kernel,
        grid=(m // tm,),
        in_specs=[
            pl.BlockSpec((tm, o.shape[1]), row),
            pl.BlockSpec((tm, d), row),
            _const_spec(w_o.shape),
            _const_spec(ln.shape),
            _const_spec(w_gu.shape),
            _const_spec(w_dn.shape),
        ],
        out_specs=pl.BlockSpec((tm, d), row),
        out_shape=jax.ShapeDtypeStruct((m, d), F32),
        scratch_shapes=[pltpu.VMEM((tm, d), BF16)],
        compiler_params=_params(("parallel",)),
        name="post_ffn",
    )(o, h, w_o, ln, w_gu, w_dn)


def _split3_bf16(x):
    hi = x.astype(BF16)
    r1 = x - hi.astype(F32)
    mid = r1.astype(BF16)
    lo = (r1 - mid.astype(F32)).astype(BF16)
    return hi, mid, lo


def _gla_proj_kernel(h_ref, wm_ref, wgl_ref, wgu_ref, bg_ref, qk_ref, v_ref, r_ref, la_ref):
    xb = h_ref[...].astype(BF16)
    g_low = _dot(xb, wgl_ref[...])
    z = _dot(g_low.astype(BF16), wgu_ref[...]) + bg_ref[...]
    log_sig = jnp.minimum(z, 0.0) - jnp.log(1.0 + jnp.exp(-jnp.abs(z)))
    la_ref[...] = log_sig / GLA_TAU
    qk_ref[...] = _dot(xb, wm_ref[:, 0:2 * GLA_HK])
    v_ref[...] = _dot(xb, wm_ref[:, 2 * GLA_HK:2 * GLA_HK + GLA_HV]).astype(BF16)
    r_ref[...] = _dot(xb, wm_ref[:, 2 * GLA_HK + GLA_HV:])


def _gla_proj(h, w_main, w_gl, w_gu, b_gate):
    m, d = h.shape
    tm = TOKEN_TILE
    row = lambda i: (i, 0)
    return pl.pallas_call(
        _gla_proj_kernel,
        grid=(m // tm,),
        in_specs=[
            pl.BlockSpec((tm, d), row),
            _const_spec(w_main.shape),
            _const_spec(w_gl.shape),
            _const_spec(w_gu.shape),
            _const_spec(b_gate.shape),
        ],
        out_specs=[
            pl.BlockSpec((tm, 2 * GLA_HK), row),
            pl.BlockSpec((tm, GLA_HV), row),
            pl.BlockSpec((tm, GLA_HV), row),
            pl.BlockSpec((tm, GLA_HK), row),
        ],
        out_shape=[
            jax.ShapeDtypeStruct((m, 2 * GLA_HK), F32),
            jax.ShapeDtypeStruct((m, GLA_HV), BF16),
            jax.ShapeDtypeStruct((m, GLA_HV), F32),
            jax.ShapeDtypeStruct((m, GLA_HK), F32),
        ],
        compiler_params=_params(("parallel",)),
        name="gla_proj",
    )(h, w_main, w_gl, w_gu, b_gate)


def _gla_core_kernel(qk_ref, v_ref, r_ref, la_ref, g_ref, y_ref, state_ref, *, n_chunks):
    slot = 0

    @pl.when(pl.program_id(1) == 0)
    def _():
        state_ref[...] = jnp.zeros_like(state_ref)

    row_i = lax.broadcasted_iota(jnp.int32, (CHUNK, CHUNK), 0)
    col_i = lax.broadcasted_iota(jnp.int32, (CHUNK, CHUNK), 1)
    causal = col_i <= row_i
    tri = causal.astype(BF16)

    def prepare(ci):
        rows = slice(ci * CHUNK, (ci + 1) * CHUNK)
        la = la_ref[slot, rows, :]
        hi, mid, lo = _split3_bf16(la)
        cum = _dot(tri, hi) + _dot(tri, mid) + _dot(tri, lo)
        tot = cum[CHUNK - 1:CHUNK, :]
        q = qk_ref[slot, rows, 0:GLA_HK] * (GLA_DK ** -0.5)
        k = qk_ref[slot, rows, GLA_HK:2 * GLA_HK]
        q_dec = (q * jnp.exp(cum)).astype(BF16)
        k_inv = (k * jnp.exp(-cum)).astype(BF16)
        k_end = (k * jnp.exp(tot - cum)).astype(BF16)
        decay = jnp.exp(tot)
        o_intra = []
        for h in range(GLA_HEADS):
            kc = slice(h * GLA_DK, (h + 1) * GLA_DK)
            vc = slice(h * GLA_DV, (h + 1) * GLA_DV)
            a = jnp.where(causal, _dot_nt(q_dec[:, kc], k_inv[:, kc]), 0.0)
            o_intra.append(_dot(a.astype(BF16), v_ref[slot, rows, vc]))
        return q_dec, k_end, decay, o_intra

    def recur(ci, prepared):
        rows = slice(ci * CHUNK, (ci + 1) * CHUNK)
        q_dec, k_end, decay, o_intra = prepared
        for h in range(GLA_HEADS):
            kc = slice(h * GLA_DK, (h + 1) * GLA_DK)
            vc = slice(h * GLA_DV, (h + 1) * GLA_DV)
            vh = v_ref[slot, rows, vc]
            st = state_ref[h]
            o = o_intra[h] + _dot_nt(q_dec[:, kc], st.astype(BF16))
            state_ref[h] = st * decay[:, kc] + _dot_tn(vh, k_end[:, kc])
            mu = jnp.mean(o, axis=-1, keepdims=True)
            oc = o - mu
            var = jnp.mean(oc * oc, axis=-1, keepdims=True)
            on = oc * lax.rsqrt(var + LN_EPS) * g_ref[:, vc]
            r = r_ref[slot, rows, vc]
            y_ref[slot, rows, vc] = (r * jax.nn.sigmoid(r) * on).astype(BF16)

    prepared = prepare(0)
    for ci in range(n_chunks):
        upcoming = prepare(ci + 1) if ci + 1 < n_chunks else None
        recur(ci, prepared)
        prepared = upcoming


def _gla_core(qk, v, r, la, norm_g, *, block):
    b, s, _ = qk.shape
    blk = lambda bi, ti: (bi, ti, 0)
    return pl.pallas_call(
        functools.partial(_gla_core_kernel, n_chunks=block // CHUNK),
        grid=(b, s // block),
        in_specs=[
            pl.BlockSpec((1, block, 2 * GLA_HK), blk),
            pl.BlockSpec((1, block, GLA_HV), blk),
            pl.BlockSpec((1, block, GLA_HV), blk),
            pl.BlockSpec((1, block, GLA_HK), blk),
            pl.BlockSpec((1, GLA_HV), lambda bi, ti: (0, 0)),
        ],
        out_specs=pl.BlockSpec((1, block, GLA_HV), blk),
        out_shape=jax.ShapeDtypeStruct((b, s, GLA_HV), BF16),
        scratch_shapes=[pltpu.VMEM((GLA_HEADS, GLA_DV, GLA_DK), F32)],
        compiler_params=_params(("parallel", "arbitrary")),
        name="gla_core",
    )(qk, v, r, la, norm_g)


def _gla_layer(h, b, s, w_in, w_gate_up, b_gate, norm_g):
    d = h.shape[1]
    n_main = 2 * GLA_HK + 2 * GLA_HV
    w_main = w_in[:, :n_main].astype(BF16)
    w_gl = jnp.zeros((d, LANES), BF16).at[:, :GLA_GATE_RANK].set(w_in[:, n_main:].astype(BF16))
    w_gu = jnp.zeros((LANES, GLA_HK), BF16).at[:GLA_GATE_RANK, :].set(w_gate_up.astype(BF16))
    qk, v, r, la = _gla_proj(h, w_main, w_gl, w_gu, b_gate.reshape(1, GLA_HK))
    sh = lambda t: t.reshape(b, s, t.shape[-1])
    y = _gla_core(sh(qk), sh(v), sh(r), sh(la), norm_g.reshape(1, GLA_HV), block=GLA_BLOCK)
    return y.reshape(b * s, GLA_HV)


def _rpa_proj_kernel(h_ref, wqt_ref, wk_ref, wvt_ref, qt_ref, k_ref, vt_ref):
    xb = h_ref[...].astype(BF16)
    tm = h_ref.shape[0]
    qt_ref[0] = (_dot_nt(wqt_ref[...], xb) * (RPA_HEAD_DIM ** -0.5 * LOG2E)).astype(BF16)
    k_ref[...] = _dot(xb, wk_ref[...]).astype(BF16)
    vt = _dot_nt(wvt_ref[...], xb).astype(BF16)
    for p in range(RPA_PAIRS):
        r0 = p * RPA_VROWS
        vt_ref[0, r0:r0 + LANES, :] = vt[p * LANES:(p + 1) * LANES]
        vt_ref[0, r0 + LANES:r0 + RPA_VROWS, :] = jnp.ones((BF16_SUBLANES, tm), BF16)


def _rpa_proj(h, b, s, w_qt, w_k, w_vt):
    m, d = h.shape
    tm = RPA_TQ
    nt = s // tm
    row = lambda i: (i, 0)
    col = lambda i: (i // nt, 0, i % nt)
    return pl.pallas_call(
        _rpa_proj_kernel,
        grid=(m // tm,),
        in_specs=[pl.BlockSpec((tm, d), row), _const_spec(w_qt.shape), _const_spec(w_k.shape),
                  _const_spec(w_vt.shape)],
        out_specs=[pl.BlockSpec((1, d, tm), col), pl.BlockSpec((tm, d), row),
                   pl.BlockSpec((1, RPA_PAIRS * RPA_VROWS, tm), col)],
        out_shape=[jax.ShapeDtypeStruct((b, d, s), BF16), jax.ShapeDtypeStruct((m, d), BF16),
                   jax.ShapeDtypeStruct((b, RPA_PAIRS * RPA_VROWS, s), BF16)],
        compiler_params=_params(("parallel",)),
        name="rpa_proj",
    )(h, w_qt, w_k, w_vt)


def _rpa_core_kernel(qt_ref, kp_ref, kc_ref, vtp_ref, vtc_ref, bias_ref, o_ref, kw_ref, vw_ref, st_ref, mx_ref,
                     e_ref):
    t = pl.program_id(1)
    tq = o_ref.shape[1]
    kw_ref[0:tq, :] = kp_ref[0]
    kw_ref[tq:2 * tq, :] = kc_ref[0]
    vw_ref[:, 0:tq] = vtp_ref[0]
    vw_ref[:, tq:2 * tq] = vtc_ref[0]

    first_head = lax.broadcasted_iota(jnp.int32, (LANES, RPA_GQ), 0) < RPA_HEAD_DIM
    key_row = lax.broadcasted_iota(jnp.int32, (RPA_GK, 2 * RPA_GQ), 0)

    def scores(slot, g, p):
        q0 = g * RPA_GQ
        feat = slice(p * LANES, (p + 1) * LANES)
        qt = qt_ref[0, feat, q0:q0 + RPA_GQ]
        zero = jnp.zeros_like(qt)
        q_blk = jnp.concatenate([jnp.where(first_head, qt, zero), jnp.where(first_head, zero, qt)], axis=1)
        st = _dot(kw_ref[q0:q0 + RPA_GK, feat], q_blk) + bias_ref[p]
        st = jnp.where(jnp.logical_and(t == 0, key_row + q0 < tq), NEG_INF, st)
        st_ref[slot] = st
        mx_ref[slot] = jnp.max(st, axis=0, keepdims=True)

    def softmax(slot, g, p):
        e_ref[slot] = jnp.exp2(st_ref[slot] - mx_ref[slot]).astype(BF16)

    def values(slot, g, p):
        q0 = g * RPA_GQ
        feat = slice(p * LANES, (p + 1) * LANES)
        ot = _dot(vw_ref[p * RPA_VROWS:(p + 1) * RPA_VROWS, q0:q0 + RPA_GK], e_ref[slot])
        on = ot[0:LANES, :] * (1.0 / ot[LANES:LANES + 1, :])
        pair_t = jnp.where(first_head, on[:, 0:RPA_GQ], on[:, RPA_GQ:2 * RPA_GQ])
        o_ref[0, q0:q0 + RPA_GQ, feat] = pair_t.T.astype(BF16)

    units = [(g, p) for g in range(tq // RPA_GQ) for p in range(RPA_PAIRS)]
    n = len(units)
    scores(0, *units[0])
    scores(1, *units[1])
    softmax(0, *units[0])
    for i in range(n):
        if i + 2 < n:
            scores(i % 2, *units[i + 2])
        if i + 1 < n:
            softmax((i + 1) % 2, *units[i + 1])
        values(i % 2, *units[i])


def _rpa_core(qt, k, vt, bias_t):
    b, s, d = k.shape
    tq = RPA_TQ
    vrows = vt.shape[1]
    prev = lambda ti: jnp.maximum(ti - 1, 0)
    return pl.pallas_call(
        _rpa_core_kernel,
        grid=(b, s // tq),
        in_specs=[
            pl.BlockSpec((1, d, tq), lambda bi, ti: (bi, 0, ti)),
            pl.BlockSpec((1, tq, d), lambda bi, ti: (bi, prev(ti), 0)),
            pl.BlockSpec((1, tq, d), lambda bi, ti: (bi, ti, 0)),
            pl.BlockSpec((1, vrows, tq), lambda bi, ti: (bi, 0, prev(ti))),
            pl.BlockSpec((1, vrows, tq), lambda bi, ti: (bi, 0, ti)),
            _const_spec(bias_t.shape),
        ],
        out_specs=pl.BlockSpec((1, tq, d), lambda bi, ti: (bi, ti, 0)),
        out_shape=jax.ShapeDtypeStruct((b, s, d), BF16),
        scratch_shapes=[pltpu.VMEM((2 * tq, d), BF16), pltpu.VMEM((vrows, 2 * tq), BF16),
                        pltpu.VMEM((2, RPA_GK, 2 * RPA_GQ), F32), pltpu.VMEM((2, 1, 2 * RPA_GQ), F32),
                        pltpu.VMEM((2, RPA_GK, 2 * RPA_GQ), BF16)],
        compiler_params=_params(("parallel", "parallel")),
        name="rpa_core",
    )(qt, k, k, vt, vt, bias_t)


def _rpa_group_bias_t(rel_bias):
    gq, gk = RPA_GQ, RPA_GK
    span = gq + gk - 1
    diag = np.arange(span)
    rel = np.clip(RPA_LEFT_CHUNKS * CHUNK + gq - 1 - diag, -RPA_MAX_REL, RPA_MAX_REL) + RPA_MAX_REL
    u = jnp.pad(rel_bias[:, rel].astype(F32), ((0, 0), (0, 1)))
    shifted = jnp.tile(u, (1, gq))[:, :gq * span].reshape(-1, gq, span)
    table = shifted[:, :, gq - 1:gq - 1 + gk]
    r = np.arange(gq)[:, None]
    w = np.arange(gk)[None, :]
    band = w - (r // CHUNK) * CHUNK
    in_band = (band >= 0) & (band < (RPA_LEFT_CHUNKS + 1) * CHUNK)
    table = jnp.where(in_band[None], table * LOG2E, NEG_INF)
    table_t = table.transpose(0, 2, 1).reshape(RPA_PAIRS, 2, gk, gq)
    return table_t.transpose(0, 2, 1, 3).reshape(RPA_PAIRS, gk, 2 * gq)


def _rpa_layer(h, b, s, w_in, rel_bias):
    d = h.shape[1]
    w = w_in.astype(BF16)
    qt, k, vt = _rpa_proj(h, b, s, w[:, 0:d].T, w[:, d:2 * d], w[:, 2 * d:3 * d].T)
    o = _rpa_core(qt, k.reshape(b, s, d), vt, _rpa_group_bias_t(rel_bias))
    return o.reshape(b * s, d)


def _rms_norm(x, g):
    return x * lax.rsqrt(jnp.mean(x * x, axis=-1, keepdims=True) + RMS_EPS) * g


def _mla_proj_kernel(h_ref, cos_ref, sin_ref, cost_ref, sint_ref, wcq_ref, wckv_ref, wkr_ref,
                     gq_ref, gkv_ref, wqt_ref, wqst_ref, wkn_ref, wvt_ref,
                     qt_ref, kn_ref, kr_ref, vt_ref):
    xb = h_ref[...].astype(BF16)
    tm = h_ref.shape[0]
    scale = MLA_QK ** -0.5 * LOG2E
    cq = _rms_norm(_dot(xb, wcq_ref[...]), gq_ref[...]).astype(BF16)
    ckv = _rms_norm(_dot(xb, wckv_ref[...]), gkv_ref[...]).astype(BF16)
    kn_ref[...] = _dot(ckv, wkn_ref[...]).astype(BF16)
    kr2 = _dot(xb, wkr_ref[...])
    kr = kr2[:, 0:LANES] * cos_ref[...] + kr2[:, LANES:2 * LANES] * sin_ref[...]
    kr_ref[...] = kr.astype(BF16)
    vt = _dot_nt(wvt_ref[...], ckv).astype(BF16)
    for h in range(MLA_HEADS):
        r0 = h * MLA_VROWS
        vt_ref[0, 0, r0:r0 + MLA_V, :] = vt[h * MLA_V:(h + 1) * MLA_V]
        vt_ref[0, 0, r0 + MLA_V:r0 + MLA_VROWS, :] = jnp.ones((BF16_SUBLANES, tm), BF16)
    a = _dot_nt(wqt_ref[...], cq)
    bsw = _dot_nt(wqst_ref[...], cq)
    cos_t = cost_ref[...]
    sin_t = sint_ref[...]
    hq = 2 * LANES
    for h in range(MLA_HEADS):
        r0 = h * hq
        qt_ref[0, 0, r0:r0 + MLA_NOPE, :] = (a[r0:r0 + MLA_NOPE] * scale).astype(BF16)
        rope = a[r0 + MLA_NOPE:r0 + MLA_QK] * cos_t + bsw[h * MLA_ROPE:(h + 1) * MLA_ROPE] * sin_t
        qt_ref[0, 0, r0 + MLA_NOPE:r0 + MLA_QK, :] = (rope * scale).astype(BF16)
        qt_ref[0, 0, r0 + MLA_QK:r0 + hq, :] = jnp.zeros((hq - MLA_QK, qt_ref.shape[3]), BF16)


def _mla_proj(h, b, s, tables, weights):
    m, d = h.shape
    tm = MLA_TK
    nt = s // tm
    row = lambda i: (i, 0)
    pos = lambda i: (i % nt, 0)
    pos_t = lambda i: (0, i % nt)
    tile_t = lambda i: (i // nt, i % nt, 0, 0)
    wide = MLA_HEADS * LANES
    return pl.pallas_call(
        _mla_proj_kernel,
        grid=(m // tm,),
        in_specs=[pl.BlockSpec((tm, d), row), pl.BlockSpec((tm, LANES), pos), pl.BlockSpec((tm, LANES), pos),
                  pl.BlockSpec((MLA_ROPE, tm), pos_t), pl.BlockSpec((MLA_ROPE, tm), pos_t)]
        + [_const_spec(w.shape) for w in weights],
        out_specs=[pl.BlockSpec((1, 1, 2 * wide, tm), tile_t), pl.BlockSpec((tm, wide), row),
                   pl.BlockSpec((tm, LANES), row), pl.BlockSpec((1, 1, MLA_HEADS * MLA_VROWS, tm), tile_t)],
        out_shape=[jax.ShapeDtypeStruct((b, nt, 2 * wide, tm), BF16), jax.ShapeDtypeStruct((m, wide), BF16),
                   jax.ShapeDtypeStruct((m, LANES), BF16),
                   jax.ShapeDtypeStruct((b, nt, MLA_HEADS * MLA_VROWS, tm), BF16)],
        compiler_params=_params(("parallel",)),
        name="mla_proj",
    )(h, *tables, *weights)


def _mla_attn_kernel(qt_ref, qnext_ref, kn_ref, kr_ref, vt_ref, o_ref, m_ref, acc_ref, st_ref, mx_ref):
    qi = pl.program_id(2)
    tq, tk = MLA_TQ, MLA_TK
    assert tq == 2 * tk
    m_ref[...] = jnp.full_like(m_ref, NEG_INF)
    acc_ref[...] = jnp.zeros_like(acc_ref)

    def scores(slot, j, q_ref=qt_ref):
        rows = pl.ds(pl.multiple_of(j * tk, tk), tk)
        kcat = jnp.concatenate([kn_ref[0, rows, :], kr_ref[0, rows, :]], axis=1)
        st_ref[slot, :, 0:tk] = _dot(kcat, q_ref[0, 0])
        st_ref[slot, :, tk:tq] = _dot(kcat, q_ref[0, 1])
        mx_ref[slot] = jnp.max(st_ref[slot, :, 0:tq], axis=0, keepdims=True)

    def consume(slot, j, first_key_chunk=None):
        st = st_ref[slot, :, 0:tq]
        if first_key_chunk is None:
            mx = mx_ref[slot]
        else:
            kchunk = first_key_chunk + lax.broadcasted_iota(jnp.int32, (tk, tq), 0) // CHUNK
            qchunk = lax.broadcasted_iota(jnp.int32, (tk, tq), 1) // CHUNK
            st = jnp.where(kchunk <= qchunk, st, NEG_INF)
            mx = jnp.max(st, axis=0, keepdims=True)
        m_prev = m_ref[...]
        m_new = jnp.maximum(m_prev, mx)
        alpha = jnp.exp2(m_prev - m_new)
        p = jnp.exp2(st - m_new).astype(BF16)
        acc_ref[...] = alpha * acc_ref[...] + _dot(vt_ref[0, j], p)
        m_ref[...] = m_new

    @pl.when(qi == 0)
    def _():
        scores(0, 0)

    def pair(i):
        j = 2 * i
        scores(1, j + 1)
        consume(0, j)
        scores(0, j + 2)
        consume(1, j + 1)

    def two_pairs(i, carry):
        pair(2 * i)
        pair(2 * i + 1)
        return carry

    lax.fori_loop(0, qi // 2, two_pairs, 0)

    @pl.when(qi % 2 == 1)
    def _():
        pair(qi - 1)

    scores(1, 2 * qi + 1)
    consume(0, 2 * qi, first_key_chunk=0)
    scores(0, 0, qnext_ref)
    consume(1, 2 * qi + 1, first_key_chunk=tk // CHUNK)

    acc = acc_ref[...]
    o_ref[0] = (acc[0:MLA_V] * (1.0 / acc[MLA_V:MLA_V + 1])).T.astype(BF16)


def _mla_attn(qt, kn, kr, vt):
    b, nt, _, tk = qt.shape
    s = kn.shape[1]
    tq = MLA_TQ
    assert MLA_TK == tk
    last_q = s // tq - 1
    return pl.pallas_call(
        _mla_attn_kernel,
        grid=(b, MLA_HEADS, s // tq),
        in_specs=[
            pl.BlockSpec((1, tq // tk, 2 * LANES, tk), lambda bi, hi, qi: (bi, qi, hi, 0)),
            pl.BlockSpec((1, tq // tk, 2 * LANES, tk), lambda bi, hi, qi: (bi, jnp.minimum(qi + 1, last_q), hi, 0)),
            pl.BlockSpec((1, s, LANES), lambda bi, hi, qi: (bi, 0, hi)),
            pl.BlockSpec((1, s, LANES), lambda bi, hi, qi: (bi, 0, 0)),
            pl.BlockSpec((1, nt, MLA_VROWS, tk), lambda bi, hi, qi: (bi, 0, hi, 0)),
        ],
        out_specs=pl.BlockSpec((1, tq, LANES), lambda bi, hi, qi: (bi, qi, hi)),
        out_shape=jax.ShapeDtypeStruct((b, s, MLA_HEADS * MLA_V), BF16),
        scratch_shapes=[
            pltpu.VMEM((1, tq), F32),
            pltpu.VMEM((MLA_VROWS, tq), F32),
            pltpu.VMEM((2, tk, tq + MLA_ST_PAD), F32),
            pltpu.VMEM((2, 1, tq), F32),
        ],
        compiler_params=_params(("parallel", "parallel", "arbitrary")),
        name="mla_attn",
    )(qt, qt, kn, kr, vt)


def _swap_halves(w):
    half = w.shape[-1] // 2
    return jnp.concatenate([w[..., half:], w[..., :half]], axis=-1)


def _mla_layer(h, b, s, w_in, q_norm_g, kv_norm_g, w_uq, w_ukv):
    pad = LANES - MLA_ROPE
    w_cq = w_in[:, :MLA_Q_RANK]
    w_ckv = w_in[:, MLA_Q_RANK:MLA_Q_RANK + MLA_KV_RANK]
    w_kr = w_in[:, MLA_Q_RANK + MLA_KV_RANK:]
    pad_cols = lambda w: jnp.pad(w, [(0, 0)] * (w.ndim - 1) + [(0, pad)])
    w_uq = w_uq.reshape(MLA_Q_RANK, MLA_HEADS, MLA_QK)
    w_qt = jnp.pad(w_uq, [(0, 0), (0, 0), (0, 2 * LANES - MLA_QK)]).reshape(MLA_Q_RANK, -1).T
    w_qst = _swap_halves(w_uq[:, :, MLA_NOPE:]).reshape(MLA_Q_RANK, -1).T
    w_ukv = w_ukv.reshape(MLA_KV_RANK, MLA_HEADS, MLA_NOPE + MLA_V)
    w_kn = w_ukv[:, :, :MLA_NOPE].reshape(MLA_KV_RANK, MLA_HEADS * MLA_NOPE)
    w_vt = w_ukv[:, :, MLA_NOPE:].reshape(MLA_KV_RANK, MLA_HEADS * MLA_V).T
    inv = ROPE_BASE ** (-jnp.arange(0, MLA_ROPE, 2, dtype=F32) / MLA_ROPE)
    ang = jnp.arange(s, dtype=F32)[:, None] * inv[None, :]
    cos, sin = jnp.cos(ang), jnp.sin(ang)
    cos2 = jnp.concatenate([cos, cos], axis=1)
    sin2 = jnp.concatenate([-sin, sin], axis=1)
    tables = [jnp.concatenate([cos2, cos2], axis=1), jnp.concatenate([sin2, sin2], axis=1), cos2.T, sin2.T]
    bf = lambda w: w.astype(BF16)
    w_kr2 = jnp.concatenate([pad_cols(w_kr), pad_cols(_swap_halves(w_kr))], axis=1)
    weights = [bf(w_cq), bf(w_ckv), bf(w_kr2),
               q_norm_g.reshape(1, -1), kv_norm_g.reshape(1, -1),
               bf(w_qt), bf(w_qst), bf(w_kn), bf(w_vt)]
    qt, kn, kr, vt = _mla_proj(h, b, s, tables, weights)
    sh = lambda t: t.reshape(b, s, t.shape[-1])
    o = _mla_attn(qt, sh(kn), sh(kr), vt)
    return o.reshape(b * s, MLA_HEADS * MLA_V)


def kernel(x, gla_w_in, gla_w_gate_up, gla_b_gate, gla_norm_g, gla_w_out,
           rpa_w_in, rpa_rel_bias, rpa_w_out,
           mla_w_in, mla_q_norm_g, mla_kv_norm_g, mla_w_uq, mla_w_ukv, mla_w_out,
           ffn_w_in, ffn_w_out, ln_g, ln_b):
    b, s, d = x.shape
    h = x.reshape(b * s, d)
    assert ffn_w_out.shape[1] % FFN_CHUNK == 0
    for i in range(DEPTH):
        m, j = i % N_MIXERS, i // N_MIXERS
        if m == 0:
            o = _gla_layer(h, b, s, gla_w_in[j], gla_w_gate_up[j], gla_b_gate[j], gla_norm_g[j])
            w_o = gla_w_out[j]
        elif m == 1:
            o = _rpa_layer(h, b, s, rpa_w_in[j], rpa_rel_bias[j])
            w_o = rpa_w_out[j]
        else:
            o = _mla_layer(h, b, s, mla_w_in[j], mla_q_norm_g[j], mla_kv_norm_g[j], mla_w_uq[j], mla_w_ukv[j])
            w_o = mla_w_out[j]
        ln = jnp.stack([ln_g[i, 0], ln_b[i, 0], ln_g[i, 1], ln_b[i, 1]])
        h = _post(o, h, w_o.astype(BF16), ln, ffn_w_in[i].astype(BF16), ffn_w_out[i].astype(BF16))
    return h.reshape(b, s, d)
```

```python
import functools

import jax
import jax.numpy as jnp
import numpy as np
from jax import lax
from jax.experimental import pallas as pl
from jax.experimental.pallas import tpu as pltpu

F32 = jnp.float32
BF16 = jnp.bfloat16

LANES = 128
BF16_SUBLANES = 16
VMEM_LIMIT = 56 * 1024 * 1024
LOG2E = 1.4426950408889634

DEPTH = 4
N_MIXERS = 3
CHUNK = 64
DEEPNORM_ALPHA = (2 * DEPTH) ** 0.25
LN_EPS = 1e-5
RMS_EPS = 1e-6
NEG_INF = -1e30

GLA_HEADS = 4
GLA_DK = 128
GLA_DV = 256
GLA_GATE_RANK = 16
GLA_TAU = 16.0
GLA_HK = GLA_HEADS * GLA_DK
GLA_HV = GLA_HEADS * GLA_DV
GLA_BLOCK = 1024

RPA_HEADS = 16
RPA_HEAD_DIM = 64
RPA_LEFT_CHUNKS = 8
RPA_MAX_REL = 128
RPA_GROUP = 2
RPA_GQ = RPA_GROUP * CHUNK
RPA_GK = (RPA_LEFT_CHUNKS + RPA_GROUP) * CHUNK
RPA_TQ = 512
RPA_PAIRS = RPA_HEADS // 2
RPA_VROWS = LANES + BF16_SUBLANES

MLA_HEADS = 8
MLA_NOPE = 128
MLA_ROPE = 64
MLA_V = 128
MLA_Q_RANK = 384
MLA_KV_RANK = 256
MLA_QK = MLA_NOPE + MLA_ROPE
ROPE_BASE = 10000.0
MLA_TQ = 1024
MLA_TK = 512
MLA_VROWS = MLA_V + BF16_SUBLANES

TOKEN_TILE = 512
POST_SUBTILES = 2
FFN_CHUNK = 256


def _params(sem):
    return pltpu.CompilerParams(dimension_semantics=sem, vmem_limit_bytes=VMEM_LIMIT)


def _const_spec(shape):
    nd = len(shape)
    return pl.BlockSpec(shape, lambda *_: (0,) * nd, pipeline_mode=pl.Buffered(1))


def _layer_norm(x, g, b):
    mu = jnp.mean(x, axis=-1, keepdims=True)
    xc = x - mu
    var = jnp.mean(xc * xc, axis=-1, keepdims=True)
    return xc * lax.rsqrt(var + LN_EPS) * g + b


def _dot(a, b):
    return jnp.dot(a, b, preferred_element_type=F32)


def _dot_nt(a, b):
    return lax.dot_general(a, b, (((1,), (1,)), ((), ())), preferred_element_type=F32)


def _dot_tn(a, b):
    return lax.dot_general(a, b, (((0,), (0,)), ((), ())), preferred_element_type=F32)


def _post_kernel(o_ref, h_ref, wo_ref, ln_ref, wgu_ref, wdn_ref, out_ref, xb_ref):
    tm = out_ref.shape[0]
    hidden = wdn_ref.shape[0]
    halves = [slice(r0, r0 + TOKEN_TILE) for r0 in range(0, tm, TOKEN_TILE)]

    def mixer_out_norm(rows):
        y = _dot(o_ref[rows, :], wo_ref[...])
        h1 = _layer_norm(DEEPNORM_ALPHA * h_ref[rows, :] + y, ln_ref[0:1, :], ln_ref[1:2, :])
        out_ref[rows, :] = h1
        xb_ref[rows, :] = h1.astype(BF16)

    def ffn_norm(rows):
        xb = xb_ref[rows, :]
        acc = None
        for c0 in range(0, hidden, FFN_CHUNK):
            g = _dot(xb, wgu_ref[:, c0:c0 + FFN_CHUNK])
            u = _dot(xb, wgu_ref[:, hidden + c0:hidden + c0 + FFN_CHUNK])
            a = (g * jax.nn.sigmoid(g) * u).astype(BF16)
            part = _dot(a, wdn_ref[c0:c0 + FFN_CHUNK, :])
            acc = part if acc is None else acc + part
        out_ref[rows, :] = _layer_norm(DEEPNORM_ALPHA * out_ref[rows, :] + acc, ln_ref[2:3, :], ln_ref[3:4, :])

    for rows in halves:
        mixer_out_norm(rows)
    for rows in halves:
        ffn_norm(rows)


def _post(o, h, w_o, ln, w_gu, w_dn):
    m, d = h.shape
    tm = POST_SUBTILES * TOKEN_TILE
    row = lambda i: (i, 0)
    return pl.pallas_call(
        _post_kernel,
        grid=(m // tm,),
        in_specs=[
            pl.BlockSpec((tm, o.shape[1]), row),
            pl.BlockSpec((tm, d), row),
            _const_spec(w_o.shape),
            _const_spec(ln.shape),
            _const_spec(w_gu.shape),
            _const_spec(w_dn.shape),
        ],
        out_specs=pl.BlockSpec((tm, d), row),
        out_shape=jax.ShapeDtypeStruct((m, d), F32),
        scratch_shapes=[pltpu.VMEM((tm, d), BF16)],
        compiler_params=_params(("parallel",)),
        name="post_ffn",
    )(o, h, w_o, ln, w_gu, w_dn)


def _split3_bf16(x):
    hi = x.astype(BF16)
    r1 = x - hi.astype(F32)
    mid = r1.astype(BF16)
    lo = (r1 - mid.astype(F32)).astype(BF16)
    return hi, mid, lo


def _gla_proj_kernel(h_ref, wm_ref, wgl_ref, wgu_ref, bg_ref, qk_ref, v_ref, r_ref, la_ref):
    xb = h_ref[...].astype(BF16)
    g_low = _dot(xb, wgl_ref[...])
    z = _dot(g_low.astype(BF16), wgu_ref[...]) + bg_ref[...]
    log_sig = jnp.minimum(z, 0.0) - jnp.log(1.0 + jnp.exp(-jnp.abs(z)))
    la_ref[...] = log_sig / GLA_TAU
    qk_ref[...] = _dot(xb, wm_ref[:, 0:2 * GLA_HK])
    v_ref[...] = _dot(xb, wm_ref[:, 2 * GLA_HK:2 * GLA_HK + GLA_HV]).astype(BF16)
    r_ref[...] = _dot(xb, wm_ref[:, 2 * GLA_HK + GLA_HV:])


def _gla_proj(h, w_main, w_gl, w_gu, b_gate):
    m, d = h.shape
    tm = TOKEN_TILE
    row = lambda i: (i, 0)
    return pl.pallas_call(
        _gla_proj_kernel,
        grid=(m // tm,),
        in_specs=[
            pl.BlockSpec((tm, d), row),
            _const_spec(w_main.shape),
            _const_spec(w_gl.shape),
            _const_spec(w_gu.shape),
            _const_spec(b_gate.shape),
        ],
        out_specs=[
            pl.BlockSpec((tm, 2 * GLA_HK), row),
            pl.BlockSpec((tm, GLA_HV), row),
            pl.BlockSpec((tm, GLA_HV), row),
            pl.BlockSpec((tm, GLA_HK), row),
        ],
        out_shape=[
            jax.ShapeDtypeStruct((m, 2 * GLA_HK), F32),
            jax.ShapeDtypeStruct((m, GLA_HV), BF16),
            jax.ShapeDtypeStruct((m, GLA_HV), F32),
            jax.ShapeDtypeStruct((m, GLA_HK), F32),
        ],
        compiler_params=_params(("parallel",)),
        name="gla_proj",
    )(h, w_main, w_gl, w_gu, b_gate)


def _gla_core_kernel(qk_ref, v_ref, r_ref, la_ref, g_ref, y_ref, state_ref, *, n_chunks):
    slot = 0

    @pl.when(pl.program_id(1) == 0)
    def _():
        state_ref[...] = jnp.zeros_like(state_ref)

    row_i = lax.broadcasted_iota(jnp.int32, (CHUNK, CHUNK), 0)
    col_i = lax.broadcasted_iota(jnp.int32, (CHUNK, CHUNK), 1)
    causal = col_i <= row_i
    tri = causal.astype(BF16)

    def prepare(ci):
        rows = slice(ci * CHUNK, (ci + 1) * CHUNK)
        la = la_ref[slot, rows, :]
        hi, mid, lo = _split3_bf16(la)
        cum = _dot(tri, hi) + _dot(tri, mid) + _dot(tri, lo)
        tot = cum[CHUNK - 1:CHUNK, :]
        q = qk_ref[slot, rows, 0:GLA_HK] * (GLA_DK ** -0.5)
        k = qk_ref[slot, rows, GLA_HK:2 * GLA_HK]
        q_dec = (q * jnp.exp(cum)).astype(BF16)
        k_inv = (k * jnp.exp(-cum)).astype(BF16)
        k_end = (k * jnp.exp(tot - cum)).astype(BF16)
        decay = jnp.exp(tot)
        o_intra = []
        for h in range(GLA_HEADS):
            kc = slice(h * GLA_DK, (h + 1) * GLA_DK)
            vc = slice(h * GLA_DV, (h + 1) * GLA_DV)
            a = jnp.where(causal, _dot_nt(q_dec[:, kc], k_inv[:, kc]), 0.0)
            o_intra.append(_dot(a.astype(BF16), v_ref[slot, rows, vc]))
        return q_dec, k_end, decay, o_intra

    def recur(ci, prepared):
        rows = slice(ci * CHUNK, (ci + 1) * CHUNK)
        q_dec, k_end, decay, o_intra = prepared
        for h in range(GLA_HEADS):
            kc = slice(h * GLA_DK, (h + 1) * GLA_DK)
            vc = slice(h * GLA_DV, (h + 1) * GLA_DV)
            vh = v_ref[slot, rows, vc]
            st = state_ref[h]
            o = o_intra[h] + _dot_nt(q_dec[:, kc], st.astype(BF16))
            state_ref[h] = st * decay[:, kc] + _dot_tn(vh, k_end[:, kc])
            mu = jnp.mean(o, axis=-1, keepdims=True)
            oc = o - mu
            var = jnp.mean(oc * oc, axis=-1, keepdims=True)
            on = oc * lax.rsqrt(var + LN_EPS) * g_ref[:, vc]
            r = r_ref[slot, rows, vc]
            y_ref[slot, rows, vc] = (r * jax.nn.sigmoid(r) * on).astype(BF16)

    prepared = prepare(0)
    for ci in range(n_chunks):
        upcoming = prepare(ci + 1) if ci + 1 < n_chunks else None
        recur(ci, prepared)
        prepared = upcoming


def _gla_core(qk, v, r, la, norm_g, *, block):
    b, s, _ = qk.shape
    blk = lambda bi, ti: (bi, ti, 0)
    return pl.pallas_call(
        functools.partial(_gla_core_kernel, n_chunks=block // CHUNK),
        grid=(b, s // block),
        in_specs=[
            pl.BlockSpec((1, block, 2 * GLA_HK), blk),
            pl.BlockSpec((1, block, GLA_HV), blk),
            pl.BlockSpec((1, block, GLA_HV), blk),
            pl.BlockSpec((1, block, GLA_HK), blk),
            pl.BlockSpec((1, GLA_HV), lambda bi, ti: (0, 0)),
        ],
        out_specs=pl.BlockSpec((1, block, GLA_HV), blk),
        out_shape=jax.ShapeDtypeStruct((b, s, GLA_HV), BF16),
        scratch_shapes=[pltpu.VMEM((GLA_HEADS, GLA_DV, GLA_DK), F32)],
        compiler_params=_params(("parallel", "arbitrary")),
        name="gla_core",
    )(qk, v, r, la, norm_g)


def _gla_layer(h, b, s, w_in, w_gate_up, b_gate, norm_g):
    d = h.shape[1]
    n_main = 2 * GLA_HK + 2 * GLA_HV
    w_main = w_in[:, :n_main].astype(BF16)
    w_gl = jnp.zeros((d, LANES), BF16).at[:, :GLA_GATE_RANK].set(w_in[:, n_main:].astype(BF16))
    w_gu = jnp.zeros((LANES, GLA_HK), BF16).at[:GLA_GATE_RANK, :].set(w_gate_up.astype(BF16))
    qk, v, r, la = _gla_proj(h, w_main, w_gl, w_gu, b_gate.reshape(1, GLA_HK))
    sh = lambda t: t.reshape(b, s, t.shape[-1])
    y = _gla_core(sh(qk), sh(v), sh(r), sh(la), norm_g.reshape(1, GLA_HV), block=GLA_BLOCK)
    return y.reshape(b * s, GLA_HV)


def _rpa_proj_kernel(h_ref, wqt_ref, wk_ref, wvt_ref, qt_ref, k_ref, vt_ref):
    xb = h_ref[...].astype(BF16)
    tm = h_ref.shape[0]
    qt_ref[0] = (_dot_nt(wqt_ref[...], xb) * (RPA_HEAD_DIM ** -0.5 * LOG2E)).astype(BF16)
    k_ref[...] = _dot(xb, wk_ref[...]).astype(BF16)
    vt = _dot_nt(wvt_ref[...], xb).astype(BF16)
    for p in range(RPA_PAIRS):
        r0 = p * RPA_VROWS
        vt_ref[0, r0:r0 + LANES, :] = vt[p * LANES:(p + 1) * LANES]
        vt_ref[0, r0 + LANES:r0 + RPA_VROWS, :] = jnp.ones((BF16_SUBLANES, tm), BF16)


def _rpa_proj(h, b, s, w_qt, w_k, w_vt):
    m, d = h.shape
    tm = RPA_TQ
    nt = s // tm
    row = lambda i: (i, 0)
    col = lambda i: (i // nt, 0, i % nt)
    return pl.pallas_call(
        _rpa_proj_kernel,
        grid=(m // tm,),
        in_specs=[pl.BlockSpec((tm, d), row), _const_spec(w_qt.shape), _const_spec(w_k.shape),
                  _const_spec(w_vt.shape)],
        out_specs=[pl.BlockSpec((1, d, tm), col), pl.BlockSpec((tm, d), row),
                   pl.BlockSpec((1, RPA_PAIRS * RPA_VROWS, tm), col)],
        out_shape=[jax.ShapeDtypeStruct((b, d, s), BF16), jax.ShapeDtypeStruct((m, d), BF16),
                   jax.ShapeDtypeStruct((b, RPA_PAIRS * RPA_VROWS, s), BF16)],
        compiler_params=_params(("parallel",)),
        name="rpa_proj",
    )(h, w_qt, w_k, w_vt)


def _rpa_core_kernel(qt_ref, kp_ref, kc_ref, vtp_ref, vtc_ref, bias_ref, o_ref, kw_ref, vw_ref, st_ref, mx_ref,
                     e_ref):
    t = pl.program_id(1)
    tq = o_ref.shape[1]
    kw_ref[0:tq, :] = kp_ref[0]
    kw_ref[tq:2 * tq, :] = kc_ref[0]
    vw_ref[:, 0:tq] = vtp_ref[0]
    vw_ref[:, tq:2 * tq] = vtc_ref[0]

    first_head = lax.broadcasted_iota(jnp.int32, (LANES, RPA_GQ), 0) < RPA_HEAD_DIM
    key_row = lax.broadcasted_iota(jnp.int32, (RPA_GK, 2 * RPA_GQ), 0)

    def scores(slot, g, p):
        q0 = g * RPA_GQ
        feat = slice(p * LANES, (p + 1) * LANES)
        qt = qt_ref[0, feat, q0:q0 + RPA_GQ]
        zero = jnp.zeros_like(qt)
        q_blk = jnp.concatenate([jnp.where(first_head, qt, zero), jnp.where(first_head, zero, qt)], axis=1)
        st = _dot(kw_ref[q0:q0 + RPA_GK, feat], q_blk) + bias_ref[p]
        st = jnp.where(jnp.logical_and(t == 0, key_row + q0 < tq), NEG_INF, st)
        st_ref[slot] = st
        mx_ref[slot] = jnp.max(st, axis=0, keepdims=True)

    def softmax(slot, g, p):
        e_ref[slot] = jnp.exp2(st_ref[slot] - mx_ref[slot]).astype(BF16)

    def values(slot, g, p):
        q0 = g * RPA_GQ
        feat = slice(p * LANES, (p + 1) * LANES)
        ot = _dot(vw_ref[p * RPA_VROWS:(p + 1) * RPA_VROWS, q0:q0 + RPA_GK], e_ref[slot])
        on = ot[0:LANES, :] * (1.0 / ot[LANES:LANES + 1, :])
        pair_t = jnp.where(first_head, on[:, 0:RPA_GQ], on[:, RPA_GQ:2 * RPA_GQ])
        o_ref[0, q0:q0 + RPA_GQ, feat] = pair_t.T.astype(BF16)

    units = [(g, p) for g in range(tq // RPA_GQ) for p in range(RPA_PAIRS)]
    n = len(units)
    scores(0, *units[0])
    scores(1, *units[1])
    softmax(0, *units[0])
    for i in range(n):
        if i + 2 < n:
            scores(i % 2, *units[i + 2])
        if i + 1 < n:
            softmax((i + 1) % 2, *units[i + 1])
        values(i % 2, *units[i])


def _rpa_core(qt, k, vt, bias_t):
    b, s, d = k.shape
    tq = RPA_TQ
    vrows = vt.shape[1]
    prev = lambda ti: jnp.maximum(ti - 1, 0)
    return pl.pallas_call(
        _rpa_core_kernel,
        grid=(b, s // tq),
        in_specs=[
            pl.BlockSpec((1, d, tq), lambda bi, ti: (bi, 0, ti)),
            pl.BlockSpec((1, tq, d), lambda bi, ti: (bi, prev(ti), 0)),
            pl.BlockSpec((1, tq, d), lambda bi, ti: (bi, ti, 0)),
            pl.BlockSpec((1, vrows, tq), lambda bi, ti: (bi, 0, prev(ti))),
            pl.BlockSpec((1, vrows, tq), lambda bi, ti: (bi, 0, ti)),
            _const_spec(bias_t.shape),
        ],
        out_specs=pl.BlockSpec((1, tq, d), lambda bi, ti: (bi, ti, 0)),
        out_shape=jax.ShapeDtypeStruct((b, s, d), BF16),
        scratch_shapes=[pltpu.VMEM((2 * tq, d), BF16), pltpu.VMEM((vrows, 2 * tq), BF16),
                        pltpu.VMEM((2, RPA_GK, 2 * RPA_GQ), F32), pltpu.VMEM((2, 1, 2 * RPA_GQ), F32),
                        pltpu.VMEM((2, RPA_GK, 2 * RPA_GQ), BF16)],
        compiler_params=_params(("parallel", "parallel")),
        name="rpa_core",
    )(qt, k, k, vt, vt, bias_t)


def _rpa_group_bias_t(rel_bias):
    gq, gk = RPA_GQ, RPA_GK
    span = gq + gk - 1
    diag = np.arange(span)
    rel = np.clip(RPA_LEFT_CHUNKS * CHUNK + gq - 1 - diag, -RPA_MAX_REL, RPA_MAX_REL) + RPA_MAX_REL
    u = jnp.pad(rel_bias[:, rel].astype(F32), ((0, 0), (0, 1)))
    shifted = jnp.tile(u, (1, gq))[:, :gq * span].reshape(-1, gq, span)
    table = shifted[:, :, gq - 1:gq - 1 + gk]
    r = np.arange(gq)[:, None]
    w = np.arange(gk)[None, :]
    band = w - (r // CHUNK) * CHUNK
    in_band = (band >= 0) & (band < (RPA_LEFT_CHUNKS + 1) * CHUNK)
    table = jnp.where(in_band[None], table * LOG2E, NEG_INF)
    table_t = table.transpose(0, 2, 1).reshape(RPA_PAIRS, 2, gk, gq)
    return table_t.transpose(0, 2, 1, 3).reshape(RPA_PAIRS, gk, 2 * gq)


def _rpa_layer(h, b, s, w_in, rel_bias):
    d = h.shape[1]
    w = w_in.astype(BF16)
    qt, k, vt = _rpa_proj(h, b, s, w[:, 0:d].T, w[:, d:2 * d], w[:, 2 * d:3 * d].T)
    o = _rpa_core(qt, k.reshape(b, s, d), vt, _rpa_group_bias_t(rel_bias))
    return o.reshape(b * s, d)


def _rms_norm(x, g):
    return x * lax.rsqrt(jnp.mean(x * x, axis=-1, keepdims=True) + RMS_EPS) * g


def _mla_proj_kernel(h_ref, cos_ref, sin_ref, cost_ref, sint_ref, wcq_ref, wckv_ref, wkr_ref,
                     gq_ref, gkv_ref, wqt_ref, wqst_ref, wkn_ref, wvt_ref,
                     qt_ref, kn_ref, kr_ref, vt_ref):
    xb = h_ref[...].astype(BF16)
    tm = h_ref.shape[0]
    scale = MLA_QK ** -0.5 * LOG2E
    cq = _rms_norm(_dot(xb, wcq_ref[...]), gq_ref[...]).astype(BF16)
    ckv = _rms_norm(_dot(xb, wckv_ref[...]), gkv_ref[...]).astype(BF16)
    kn_ref[...] = _dot(ckv, wkn_ref[...]).astype(BF16)
    kr2 = _dot(xb, wkr_ref[...])
    kr = kr2[:, 0:LANES] * cos_ref[...] + kr2[:, LANES:2 * LANES] * sin_ref[...]
    kr_ref[...] = kr.astype(BF16)
    vt = _dot_nt(wvt_ref[...], ckv).astype(BF16)
    for h in range(MLA_HEADS):
        r0 = h * MLA_VROWS
        vt_ref[0, 0, r0:r0 + MLA_V, :] = vt[h * MLA_V:(h + 1) * MLA_V]
        vt_ref[0, 0, r0 + MLA_V:r0 + MLA_VROWS, :] = jnp.ones((BF16_SUBLANES, tm), BF16)
    a = _dot_nt(wqt_ref[...], cq)
    bsw = _dot_nt(wqst_ref[...], cq)
    cos_t = cost_ref[...]
    sin_t = sint_ref[...]
    hq = 2 * LANES
    for h in range(MLA_HEADS):
        r0 = h * hq
        qt_ref[0, 0, r0:r0 + MLA_NOPE, :] = (a[r0:r0 + MLA_NOPE] * scale).astype(BF16)
        rope = a[r0 + MLA_NOPE:r0 + MLA_QK] * cos_t + bsw[h * MLA_ROPE:(h + 1) * MLA_ROPE] * sin_t
        qt_ref[0, 0, r0 + MLA_NOPE:r0 + MLA_QK, :] = (rope * scale).astype(BF16)
        qt_ref[0, 0, r0 + MLA_QK:r0 + hq, :] = jnp.zeros((hq - MLA_QK, qt_ref.shape[3]), BF16)


def _mla_proj(h, b, s, tables, weights):
    m, d = h.shape
    tm = MLA_TK
    nt = s // tm
    row = lambda i: (i, 0)
    pos = lambda i: (i % nt, 0)
    pos_t = lambda i: (0, i % nt)
    tile_t = lambda i: (i // nt, i % nt, 0, 0)
    wide = MLA_HEADS * LANES
    return pl.pallas_call(
        _mla_proj_kernel,
        grid=(m // tm,),
        in_specs=[pl.BlockSpec((tm, d), row), pl.BlockSpec((tm, LANES), pos), pl.BlockSpec((tm, LANES), pos),
                  pl.BlockSpec((MLA_ROPE, tm), pos_t), pl.BlockSpec((MLA_ROPE, tm), pos_t)]
        + [_const_spec(w.shape) for w in weights],
        out_specs=[pl.BlockSpec((1, 1, 2 * wide, tm), tile_t), pl.BlockSpec((tm, wide), row),
                   pl.BlockSpec((tm, LANES), row), pl.BlockSpec((1, 1, MLA_HEADS * MLA_VROWS, tm), tile_t)],
        out_shape=[jax.ShapeDtypeStruct((b, nt, 2 * wide, tm), BF16), jax.ShapeDtypeStruct((m, wide), BF16),
                   jax.ShapeDtypeStruct((m, LANES), BF16),
                   jax.ShapeDtypeStruct((b, nt, MLA_HEADS * MLA_VROWS, tm), BF16)],
        compiler_params=_params(("parallel",)),
        name="mla_proj",
    )(h, *tables, *weights)


def _mla_attn_kernel(qt_ref, qnext_ref, kn_ref, kr_ref, vt_ref, o_ref, m_ref, acc_ref, st_ref, mx_ref):
    qi = pl.program_id(2)
    tq, tk = MLA_TQ, MLA_TK
    assert tq == 2 * tk
    m_ref[...] = jnp.full_like(m_ref, NEG_INF)
    acc_ref[...] = jnp.zeros_like(acc_ref)

    def scores(slot, j, q_ref=qt_ref):
        rows = pl.ds(pl.multiple_of(j * tk, tk), tk)
        kcat = jnp.concatenate([kn_ref[0, rows, :], kr_ref[0, rows, :]], axis=1)
        st_ref[slot, :, 0:tk] = _dot(kcat, q_ref[0, 0])
        st_ref[slot, :, tk:tq] = _dot(kcat, q_ref[0, 1])
        mx_ref[slot] = jnp.max(st_ref[slot], axis=0, keepdims=True)

    def consume(slot, j, first_key_chunk=None):
        st = st_ref[slot]
        if first_key_chunk is None:
            mx = mx_ref[slot]
        else:
            kchunk = first_key_chunk + lax.broadcasted_iota(jnp.int32, (tk, tq), 0) // CHUNK
            qchunk = lax.broadcasted_iota(jnp.int32, (tk, tq), 1) // CHUNK
            st = jnp.where(kchunk <= qchunk, st, NEG_INF)
            mx = jnp.max(st, axis=0, keepdims=True)
        m_prev = m_ref[...]
        m_new = jnp.maximum(m_prev, mx)
        alpha = jnp.exp2(m_prev - m_new)
        p = jnp.exp2(st - m_new).astype(BF16)
        acc_ref[...] = alpha * acc_ref[...] + _dot(vt_ref[0, j], p)
        m_ref[...] = m_new

    @pl.when(qi == 0)
    def _():
        scores(0, 0)

    def pair(i):
        j = 2 * i
        scores(1, j + 1)
        consume(0, j)
        scores(0, j + 2)
        consume(1, j + 1)

    def two_pairs(i, carry):
        pair(2 * i)
        pair(2 * i + 1)
        return carry

    lax.fori_loop(0, qi // 2, two_pairs, 0)

    @pl.when(qi % 2 == 1)
    def _():
        pair(qi - 1)

    scores(1, 2 * qi + 1)
    consume(0, 2 * qi, first_key_chunk=0)
    scores(0, 0, qnext_ref)
    consume(1, 2 * qi + 1, first_key_chunk=tk // CHUNK)

    acc = acc_ref[...]
    o_ref[0] = (acc[0:MLA_V] * (1.0 / acc[MLA_V:MLA_V + 1])).T.astype(BF16)


def _mla_attn(qt, kn, kr, vt):
    b, nt, _, tk = qt.shape
    s = kn.shape[1]
    tq = MLA_TQ
    assert MLA_TK == tk
    last_q = s // tq - 1
    return pl.pallas_call(
        _mla_attn_kernel,
        grid=(b, MLA_HEADS, s // tq),
        in_specs=[
            pl.BlockSpec((1, tq // tk, 2 * LANES, tk), lambda bi, hi, qi: (bi, qi, hi, 0)),
            pl.BlockSpec((1, tq // tk, 2 * LANES, tk), lambda bi, hi, qi: (bi, jnp.minimum(qi + 1, last_q), hi, 0)),
            pl.BlockSpec((1, s, LANES), lambda bi, hi, qi: (bi, 0, hi)),
            pl.BlockSpec((1, s, LANES), lambda bi, hi, qi: (bi, 0, 0)),
            pl.BlockSpec((1, nt, MLA_VROWS, tk), lambda bi, hi, qi: (bi, 0, hi, 0)),
        ],
        out_specs=pl.BlockSpec((1, tq, LANES), lambda bi, hi, qi: (bi, qi, hi)),
        out_shape=jax.ShapeDtypeStruct((b, s, MLA_HEADS * MLA_V), BF16),
        scratch_shapes=[
            pltpu.VMEM((1, tq), F32),
            pltpu.VMEM((MLA_VROWS, tq), F32),
            pltpu.VMEM((2, tk, tq), F32),
            pltpu.VMEM((2, 1, tq), F32),
        ],
        compiler_params=_params(("parallel", "parallel", "arbitrary")),
        name="mla_attn",
    )(qt, qt, kn, kr, vt)


def _swap_halves(w):
    half = w.shape[-1] // 2
    return jnp.concatenate([w[..., half:], w[..., :half]], axis=-1)


def _mla_layer(h, b, s, w_in, q_norm_g, kv_norm_g, w_uq, w_ukv):
    pad = LANES - MLA_ROPE
    w_cq = w_in[:, :MLA_Q_RANK]
    w_ckv = w_in[:, MLA_Q_RANK:MLA_Q_RANK + MLA_KV_RANK]
    w_kr = w_in[:, MLA_Q_RANK + MLA_KV_RANK:]
    pad_cols = lambda w: jnp.pad(w, [(0, 0)] * (w.ndim - 1) + [(0, pad)])
    w_uq = w_uq.reshape(MLA_Q_RANK, MLA_HEADS, MLA_QK)
    w_qt = jnp.pad(w_uq, [(0, 0), (0, 0), (0, 2 * LANES - MLA_QK)]).reshape(MLA_Q_RANK, -1).T
    w_qst = _swap_halves(w_uq[:, :, MLA_NOPE:]).reshape(MLA_Q_RANK, -1).T
    w_ukv = w_ukv.reshape(MLA_KV_RANK, MLA_HEADS, MLA_NOPE + MLA_V)
    w_kn = w_ukv[:, :, :MLA_NOPE].reshape(MLA_KV_RANK, MLA_HEADS * MLA_NOPE)
    w_vt = w_ukv[:, :, MLA_NOPE:].reshape(MLA_KV_RANK, MLA_HEADS * MLA_V).T
    inv = ROPE_BASE ** (-jnp.arange(0, MLA_ROPE, 2, dtype=F32) / MLA_ROPE)
    ang = jnp.arange(s, dtype=F32)[:, None] * inv[None, :]
    cos, sin = jnp.cos(ang), jnp.sin(ang)
    cos2 = jnp.concatenate([cos, cos], axis=1)
    sin2 = jnp.concatenate([-sin, sin], axis=1)
    tables = [jnp.concatenate([cos2, cos2], axis=1), jnp.concatenate([sin2, sin2], axis=1), cos2.T, sin2.T]
    bf = lambda w: w.astype(BF16)
    w_kr2 = jnp.concatenate([pad_cols(w_kr), pad_cols(_swap_halves(w_kr))], axis=1)
    weights = [bf(w_cq), bf(w_ckv), bf(w_kr2),
               q_norm_g.reshape(1, -1), kv_norm_g.reshape(1, -1),
               bf(w_qt), bf(w_qst), bf(w_kn), bf(w_vt)]
    qt, kn, kr, vt = _mla_proj(h, b, s, tables, weights)
    sh = lambda t: t.reshape(b, s, t.shape[-1])
    o = _mla_attn(qt, sh(kn), sh(kr), vt)
    return o.reshape(b * s, MLA_HEADS * MLA_V)


def kernel(x, gla_w_in, gla_w_gate_up, gla_b_gate, gla_norm_g, gla_w_out,
           rpa_w_in, rpa_rel_bias, rpa_w_out,
           mla_w_in, mla_q_norm_g, mla_kv_norm_g, mla_w_uq, mla_w_ukv, mla_w_out,
           ffn_w_in, ffn_w_out, ln_g, ln_b):
    b, s, d = x.shape
    h = x.reshape(b * s, d)
    assert ffn_w_out.shape[1] % FFN_CHUNK == 0
    for i in range(DEPTH):
        m, j = i % N_MIXERS, i // N_MIXERS
        if m == 0:
            o = _gla_layer(h, b, s, gla_w_in[j], gla_w_gate_up[j], gla_b_gate[j], gla_norm_g[j])
            w_o = gla_w_out[j]
        elif m == 1:
            o = _rpa_layer(h, b, s, rpa_w_in[j], rpa_rel_bias[j])
            w_o = rpa_w_out[j]
        else:
            o = _mla_layer(h, b, s, mla_w_in[j], mla_q_norm_g[j], mla_kv_norm_g[j], mla_w_uq[j], mla_w_ukv[j])
            w_o = mla_w_out[j]
        ln = jnp.stack([ln_g[i, 0], ln_b[i, 0], ln_g[i, 1], ln_b[i, 1]])
        h = _post(o, h, w_o.astype(BF16), ln, ffn_w_in[i].astype(BF16), ffn_w_out[i].astype(BF16))
    return h.reshape(b, s, d)
```

```python
import functools

import jax
import jax.numpy as jnp
import numpy as np
from jax import lax
from jax.experimental import pallas as pl
from jax.experimental.pallas import tpu as pltpu

F32 = jnp.float32
BF16 = jnp.bfloat16

LANES = 128
BF16_SUBLANES = 16
VMEM_LIMIT = 56 * 1024 * 1024
LOG2E = 1.4426950408889634

DEPTH = 4
N_MIXERS = 3
CHUNK = 64
DEEPNORM_ALPHA = (2 * DEPTH) ** 0.25
LN_EPS = 1e-5
RMS_EPS = 1e-6
NEG_INF = -1e30

GLA_HEADS = 4
GLA_DK = 128
GLA_DV = 256
GLA_GATE_RANK = 16
GLA_TAU = 16.0
GLA_HK = GLA_HEADS * GLA_DK
GLA_HV = GLA_HEADS * GLA_DV
GLA_BLOCK = 1024

RPA_HEADS = 16
RPA_HEAD_DIM = 64
RPA_LEFT_CHUNKS = 8
RPA_MAX_REL = 128
RPA_GROUP = 2
RPA_GQ = RPA_GROUP * CHUNK
RPA_GK = (RPA_LEFT_CHUNKS + RPA_GROUP) * CHUNK
RPA_TQ = 512
RPA_PAIRS = RPA_HEADS // 2
RPA_VROWS = LANES + BF16_SUBLANES

MLA_HEADS = 8
MLA_NOPE = 128
MLA_ROPE = 64
MLA_V = 128
MLA_Q_RANK = 384
MLA_KV_RANK = 256
MLA_QK = MLA_NOPE + MLA_ROPE
ROPE_BASE = 10000.0
MLA_TQ = 1024
MLA_TK = 512
MLA_VROWS = MLA_V + BF16_SUBLANES

TOKEN_TILE = 512
POST_SUBTILES = 2
FFN_CHUNK = 256


def _params(sem):
    return pltpu.CompilerParams(dimension_semantics=sem, vmem_limit_bytes=VMEM_LIMIT)


def _const_spec(shape):
    nd = len(shape)
    return pl.BlockSpec(shape, lambda *_: (0,) * nd, pipeline_mode=pl.Buffered(1))


def _layer_norm(x, g, b):
    mu = jnp.mean(x, axis=-1, keepdims=True)
    xc = x - mu
    var = jnp.mean(xc * xc, axis=-1, keepdims=True)
    return xc * lax.rsqrt(var + LN_EPS) * g + b


def _dot(a, b):
    return jnp.dot(a, b, preferred_element_type=F32)


def _dot_nt(a, b):
    return lax.dot_general(a, b, (((1,), (1,)), ((), ())), preferred_element_type=F32)


def _dot_tn(a, b):
    return lax.dot_general(a, b, (((0,), (0,)), ((), ())), preferred_element_type=F32)


def _post_kernel(o_ref, h_ref, wo_ref, ln_ref, wgu_ref, wdn_ref, out_ref, xb_ref):
    tm = out_ref.shape[0]
    hidden = wdn_ref.shape[0]
    halves = [slice(r0, r0 + TOKEN_TILE) for r0 in range(0, tm, TOKEN_TILE)]

    n_piece = 8
    pr = TOKEN_TILE // n_piece

    def piece(rows, i):
        return slice(rows.start + i * pr, rows.start + (i + 1) * pr)

    def stage_norm1(rows):
        out_ref[rows, :] = DEEPNORM_ALPHA * h_ref[rows, :] + _dot(o_ref[rows, :], wo_ref[...])

    def anchored_norm(x, g, b, anchor):
        mu = jnp.mean(x, axis=-1, keepdims=True)
        if anchor is not None:
            mu = mu + 0.0 * anchor
        xc = x - mu
        var = jnp.mean(xc * xc, axis=-1, keepdims=True)
        return xc * lax.rsqrt(var + LN_EPS) * g + b

    def norm1_piece(rows, i, anchor=None):
        r = piece(rows, i)
        h1 = anchored_norm(out_ref[r, :], ln_ref[0:1, :], ln_ref[1:2, :], anchor)
        out_ref[r, :] = h1
        xb_ref[r, :] = h1.astype(BF16)

    def norm2_piece(rows, i, anchor=None):
        r = piece(rows, i)
        out_ref[r, :] = anchored_norm(out_ref[r, :], ln_ref[2:3, :], ln_ref[3:4, :], anchor)

    def ffn(rows, after_chunk):
        xb = xb_ref[rows, :]
        acc = None
        for ci, c0 in enumerate(range(0, hidden, FFN_CHUNK)):
            g = _dot(xb, wgu_ref[:, c0:c0 + FFN_CHUNK])
            u = _dot(xb, wgu_ref[:, hidden + c0:hidden + c0 + FFN_CHUNK])
            a = (g * jax.nn.sigmoid(g) * u).astype(BF16)
            part = _dot(a, wdn_ref[c0:c0 + FFN_CHUNK, :])
            acc = part if acc is None else acc + part
            if ci < n_piece:
                after_chunk(ci, part[0:pr, 0:1])
        out_ref[rows, :] = DEEPNORM_ALPHA * out_ref[rows, :] + acc

    assert len(halves) == 2 and hidden // FFN_CHUNK >= n_piece
    first, second = halves
    stage_norm1(first)
    for i in range(n_piece):
        norm1_piece(first, i)
    stage_norm1(second)
    ffn(first, functools.partial(norm1_piece, second))
    ffn(second, functools.partial(norm2_piece, first))
    for i in range(n_piece):
        norm2_piece(second, i)


def _post(o, h, w_o, ln, w_gu, w_dn):
    m, d = h.shape
    tm = POST_SUBTILES * TOKEN_TILE
    row = lambda i: (i, 0)
    return pl.pallas_call(
        _post_kernel,
        grid=(m // tm,),
        in_specs=[
            pl.BlockSpec((tm, o.shape[1]), row),
            pl.BlockSpec((tm, d), row),
            _const_spec(w_o.shape),
            _const_spec(ln.shape),
            _const_spec(w_gu.shape),
            _const_spec(w_dn.shape),
        ],
        out_specs=pl.BlockSpec((tm, d), row),
        out_shape=jax.ShapeDtypeStruct((m, d), F32),
        scratch_shapes=[pltpu.VMEM((tm, d), BF16)],
        compiler_params=_params(("parallel",)),
        name="post_ffn",
    )(o, h, w_o, ln, w_gu, w_dn)


def _split3_bf16(x):
    hi = x.astype(BF16)
    r1 = x - hi.astype(F32)
    mid = r1.astype(BF16)
    lo = (r1 - mid.astype(F32)).astype(BF16)
    return hi, mid, lo


def _gla_proj_kernel(h_ref, wm_ref, wgl_ref, wgu_ref, bg_ref, qk_ref, v_ref, r_ref, la_ref):
    xb = h_ref[...].astype(BF16)
    g_low = _dot(xb, wgl_ref[...])
    z = _dot(g_low.astype(BF16), wgu_ref[...]) + bg_ref[...]
    log_sig = jnp.minimum(z, 0.0) - jnp.log(1.0 + jnp.exp(-jnp.abs(z)))
    la_ref[...] = log_sig / GLA_TAU
    qk_ref[...] = _dot(xb, wm_ref[:, 0:2 * GLA_HK])
    v_ref[...] = _dot(xb, wm_ref[:, 2 * GLA_HK:2 * GLA_HK + GLA_HV]).astype(BF16)
    r_ref[...] = _dot(xb, wm_ref[:, 2 * GLA_HK + GLA_HV:])


def _gla_proj(h, w_main, w_gl, w_gu, b_gate):
    m, d = h.shape
    tm = TOKEN_TILE
    row = lambda i: (i, 0)
    return pl.pallas_call(
        _gla_proj_kernel,
        grid=(m // tm,),
        in_specs=[
            pl.BlockSpec((tm, d), row),
            _const_spec(w_main.shape),
            _const_spec(w_gl.shape),
            _const_spec(w_gu.shape),
            _const_spec(b_gate.shape),
        ],
        out_specs=[
            pl.BlockSpec((tm, 2 * GLA_HK), row),
            pl.BlockSpec((tm, GLA_HV), row),
            pl.BlockSpec((tm, GLA_HV), row),
            pl.BlockSpec((tm, GLA_HK), row),
        ],
        out_shape=[
            jax.ShapeDtypeStruct((m, 2 * GLA_HK), F32),
            jax.ShapeDtypeStruct((m, GLA_HV), BF16),
            jax.ShapeDtypeStruct((m, GLA_HV), F32),
            jax.ShapeDtypeStruct((m, GLA_HK), F32),
        ],
        compiler_params=_params(("parallel",)),
        name="gla_proj",
    )(h, w_main, w_gl, w_gu, b_gate)


def _gla_core_kernel(qk_ref, v_ref, r_ref, la_ref, g_ref, y_ref, state_ref, *, n_chunks):
    slot = 0

    @pl.when(pl.program_id(1) == 0)
    def _():
        state_ref[...] = jnp.zeros_like(state_ref)

    row_i = lax.broadcasted_iota(jnp.int32, (CHUNK, CHUNK), 0)
    col_i = lax.broadcasted_iota(jnp.int32, (CHUNK, CHUNK), 1)
    causal = col_i <= row_i
    tri = causal.astype(BF16)

    def prepare(ci):
        rows = slice(ci * CHUNK, (ci + 1) * CHUNK)
        la = la_ref[slot, rows, :]
        hi, mid, lo = _split3_bf16(la)
        cum = _dot(tri, hi) + _dot(tri, mid) + _dot(tri, lo)
        tot = cum[CHUNK - 1:CHUNK, :]
        q = qk_ref[slot, rows, 0:GLA_HK] * (GLA_DK ** -0.5)
        k = qk_ref[slot, rows, GLA_HK:2 * GLA_HK]
        q_dec = (q * jnp.exp(cum)).astype(BF16)
        k_inv = (k * jnp.exp(-cum)).astype(BF16)
        k_end = (k * jnp.exp(tot - cum)).astype(BF16)
        decay = jnp.exp(tot)
        o_intra = []
        for h in range(GLA_HEADS):
            kc = slice(h * GLA_DK, (h + 1) * GLA_DK)
            vc = slice(h * GLA_DV, (h + 1) * GLA_DV)
            a = jnp.where(causal, _dot_nt(q_dec[:, kc], k_inv[:, kc]), 0.0)
            o_intra.append(_dot(a.astype(BF16), v_ref[slot, rows, vc]))
        return q_dec, k_end, decay, o_intra

    def recur(ci, prepared):
        rows = slice(ci * CHUNK, (ci + 1) * CHUNK)
        q_dec, k_end, decay, o_intra = prepared
        for h in range(GLA_HEADS):
            kc = slice(h * GLA_DK, (h + 1) * GLA_DK)
            vc = slice(h * GLA_DV, (h + 1) * GLA_DV)
            vh = v_ref[slot, rows, vc]
            st = state_ref[h]
            o = o_intra[h] + _dot_nt(q_dec[:, kc], st.astype(BF16))
            state_ref[h] = st * decay[:, kc] + _dot_tn(vh, k_end[:, kc])
            mu = jnp.mean(o, axis=-1, keepdims=True)
            oc = o - mu
            var = jnp.mean(oc * oc, axis=-1, keepdims=True)
            on = oc * lax.rsqrt(var + LN_EPS) * g_ref[:, vc]
            r = r_ref[slot, rows, vc]
            y_ref[slot, rows, vc] = (r * jax.nn.sigmoid(r) * on).astype(BF16)

    prepared = prepare(0)
    for ci in range(n_chunks):
        upcoming = prepare(ci + 1) if ci + 1 < n_chunks else None
        recur(ci, prepared)
        prepared = upcoming


def _gla_core(qk, v, r, la, norm_g, *, block):
    b, s, _ = qk.shape
    blk = lambda bi, ti: (bi, ti, 0)
    return pl.pallas_call(
        functools.partial(_gla_core_kernel, n_chunks=block // CHUNK),
        grid=(b, s // block),
        in_specs=[
            pl.BlockSpec((1, block, 2 * GLA_HK), blk),
            pl.BlockSpec((1, block, GLA_HV), blk),
            pl.BlockSpec((1, block, GLA_HV), blk),
            pl.BlockSpec((1, block, GLA_HK), blk),
            pl.BlockSpec((1, GLA_HV), lambda bi, ti: (0, 0)),
        ],
        out_specs=pl.BlockSpec((1, block, GLA_HV), blk),
        out_shape=jax.ShapeDtypeStruct((b, s, GLA_HV), BF16),
        scratch_shapes=[pltpu.VMEM((GLA_HEADS, GLA_DV, GLA_DK), F32)],
        compiler_params=_params(("parallel", "arbitrary")),
        name="gla_core",
    )(qk, v, r, la, norm_g)


def _gla_layer(h, b, s, w_in, w_gate_up, b_gate, norm_g):
    d = h.shape[1]
    n_main = 2 * GLA_HK + 2 * GLA_HV
    w_main = w_in[:, :n_main].astype(BF16)
    w_gl = jnp.zeros((d, LANES), BF16).at[:, :GLA_GATE_RANK].set(w_in[:, n_main:].astype(BF16))
    w_gu = jnp.zeros((LANES, GLA_HK), BF16).at[:GLA_GATE_RANK, :].set(w_gate_up.astype(BF16))
    qk, v, r, la = _gla_proj(h, w_main, w_gl, w_gu, b_gate.reshape(1, GLA_HK))
    sh = lambda t: t.reshape(b, s, t.shape[-1])
    y = _gla_core(sh(qk), sh(v), sh(r), sh(la), norm_g.reshape(1, GLA_HV), block=GLA_BLOCK)
    return y.reshape(b * s, GLA_HV)


def _rpa_proj_kernel(h_ref, wqt_ref, wk_ref, wvt_ref, qt_ref, k_ref, vt_ref):
    xb = h_ref[...].astype(BF16)
    tm = h_ref.shape[0]
    qt_ref[0] = (_dot_nt(wqt_ref[...], xb) * (RPA_HEAD_DIM ** -0.5 * LOG2E)).astype(BF16)
    k_ref[...] = _dot(xb, wk_ref[...]).astype(BF16)
    vt = _dot_nt(wvt_ref[...], xb).astype(BF16)
    for p in range(RPA_PAIRS):
        r0 = p * RPA_VROWS
        vt_ref[0, r0:r0 + LANES, :] = vt[p * LANES:(p + 1) * LANES]
        vt_ref[0, r0 + LANES:r0 + RPA_VROWS, :] = jnp.ones((BF16_SUBLANES, tm), BF16)


def _rpa_proj(h, b, s, w_qt, w_k, w_vt):
    m, d = h.shape
    tm = RPA_TQ
    nt = s // tm
    row = lambda i: (i, 0)
    col = lambda i: (i // nt, 0, i % nt)
    return pl.pallas_call(
        _rpa_proj_kernel,
        grid=(m // tm,),
        in_specs=[pl.BlockSpec((tm, d), row), _const_spec(w_qt.shape), _const_spec(w_k.shape),
                  _const_spec(w_vt.shape)],
        out_specs=[pl.BlockSpec((1, d, tm), col), pl.BlockSpec((tm, d), row),
                   pl.BlockSpec((1, RPA_PAIRS * RPA_VROWS, tm), col)],
        out_shape=[jax.ShapeDtypeStruct((b, d, s), BF16), jax.ShapeDtypeStruct((m, d), BF16),
                   jax.ShapeDtypeStruct((b, RPA_PAIRS * RPA_VROWS, s), BF16)],
        compiler_params=_params(("parallel",)),
        name="rpa_proj",
    )(h, w_qt, w_k, w_vt)


def _rpa_core_kernel(qt_ref, kp_ref, kc_ref, vtp_ref, vtc_ref, bias_ref, o_ref, kw_ref, vw_ref, st_ref, mx_ref,
                     e_ref):
    t = pl.program_id(1)
    tq = o_ref.shape[1]
    kw_ref[0:tq, :] = kp_ref[0]
    kw_ref[tq:2 * tq, :] = kc_ref[0]
    vw_ref[:, 0:tq] = vtp_ref[0]
    vw_ref[:, tq:2 * tq] = vtc_ref[0]

    first_head = lax.broadcasted_iota(jnp.int32, (LANES, RPA_GQ), 0) < RPA_HEAD_DIM
    key_row = lax.broadcasted_iota(jnp.int32, (RPA_GK, 2 * RPA_GQ), 0)

    def scores(slot, g, p):
        q0 = g * RPA_GQ
        feat = slice(p * LANES, (p + 1) * LANES)
        qt = qt_ref[0, feat, q0:q0 + RPA_GQ]
        zero = jnp.zeros_like(qt)
        q_blk = jnp.concatenate([jnp.where(first_head, qt, zero), jnp.where(first_head, zero, qt)], axis=1)
        st = _dot(kw_ref[q0:q0 + RPA_GK, feat], q_blk) + bias_ref[p]
        st = jnp.where(jnp.logical_and(t == 0, key_row + q0 < tq), NEG_INF, st)
        st_ref[slot] = st
        mx_ref[slot] = jnp.max(st, axis=0, keepdims=True)

    def softmax(slot, g, p):
        e_ref[slot] = jnp.exp2(st_ref[slot] - mx_ref[slot]).astype(BF16)

    def values(slot, g, p):
        q0 = g * RPA_GQ
        feat = slice(p * LANES, (p + 1) * LANES)
        ot = _dot(vw_ref[p * RPA_VROWS:(p + 1) * RPA_VROWS, q0:q0 + RPA_GK], e_ref[slot])
        on = ot[0:LANES, :] * (1.0 / ot[LANES:LANES + 1, :])
        pair_t = jnp.where(first_head, on[:, 0:RPA_GQ], on[:, RPA_GQ:2 * RPA_GQ])
        o_ref[0, q0:q0 + RPA_GQ, feat] = pair_t.T.astype(BF16)

    units = [(g, p) for g in range(tq // RPA_GQ) for p in range(RPA_PAIRS)]
    n = len(units)
    scores(0, *units[0])
    scores(1, *units[1])
    softmax(0, *units[0])
    for i in range(n):
        if i + 2 < n:
            scores(i % 2, *units[i + 2])
        if i + 1 < n:
            softmax((i + 1) % 2, *units[i + 1])
        values(i % 2, *units[i])


def _rpa_core(qt, k, vt, bias_t):
    b, s, d = k.shape
    tq = RPA_TQ
    vrows = vt.shape[1]
    prev = lambda ti: jnp.maximum(ti - 1, 0)
    return pl.pallas_call(
        _rpa_core_kernel,
        grid=(b, s // tq),
        in_specs=[
            pl.BlockSpec((1, d, tq), lambda bi, ti: (bi, 0, ti)),
            pl.BlockSpec((1, tq, d), lambda bi, ti: (bi, prev(ti), 0)),
            pl.BlockSpec((1, tq, d), lambda bi, ti: (bi, ti, 0)),
            pl.BlockSpec((1, vrows, tq), lambda bi, ti: (bi, 0, prev(ti))),
            pl.BlockSpec((1, vrows, tq), lambda bi, ti: (bi, 0, ti)),
            _const_spec(bias_t.shape),
        ],
        out_specs=pl.BlockSpec((1, tq, d), lambda bi, ti: (bi, ti, 0)),
        out_shape=jax.ShapeDtypeStruct((b, s, d), BF16),
        scratch_shapes=[pltpu.VMEM((2 * tq, d), BF16), pltpu.VMEM((vrows, 2 * tq), BF16),
                        pltpu.VMEM((2, RPA_GK, 2 * RPA_GQ), F32), pltpu.VMEM((2, 1, 2 * RPA_GQ), F32),
                        pltpu.VMEM((2, RPA_GK, 2 * RPA_GQ), BF16)],
        compiler_params=_params(("parallel", "parallel")),
        name="rpa_core",
    )(qt, k, k, vt, vt, bias_t)


def _rpa_group_bias_t(rel_bias):
    gq, gk = RPA_GQ, RPA_GK
    span = gq + gk - 1
    diag = np.arange(span)
    rel = np.clip(RPA_LEFT_CHUNKS * CHUNK + gq - 1 - diag, -RPA_MAX_REL, RPA_MAX_REL) + RPA_MAX_REL
    u = jnp.pad(rel_bias[:, rel].astype(F32), ((0, 0), (0, 1)))
    shifted = jnp.tile(u, (1, gq))[:, :gq * span].reshape(-1, gq, span)
    table = shifted[:, :, gq - 1:gq - 1 + gk]
    r = np.arange(gq)[:, None]
    w = np.arange(gk)[None, :]
    band = w - (r // CHUNK) * CHUNK
    in_band = (band >= 0) & (band < (RPA_LEFT_CHUNKS + 1) * CHUNK)
    table = jnp.where(in_band[None], table * LOG2E, NEG_INF)
    table_t = table.transpose(0, 2, 1).reshape(RPA_PAIRS, 2, gk, gq)
    return table_t.transpose(0, 2, 1, 3).reshape(RPA_PAIRS, gk, 2 * gq)


def _rpa_layer(h, b, s, w_in, rel_bias):
    d = h.shape[1]
    w = w_in.astype(BF16)
    qt, k, vt = _rpa_proj(h, b, s, w[:, 0:d].T, w[:, d:2 * d], w[:, 2 * d:3 * d].T)
    o = _rpa_core(qt, k.reshape(b, s, d), vt, _rpa_group_bias_t(rel_bias))
    return o.reshape(b * s, d)


def _rms_norm(x, g):
    return x * lax.rsqrt(jnp.mean(x * x, axis=-1, keepdims=True) + RMS_EPS) * g


def _mla_proj_kernel(h_ref, cos_ref, sin_ref, cost_ref, sint_ref, wcq_ref, wckv_ref, wkr_ref,
                     gq_ref, gkv_ref, wqt_ref, wqst_ref, wkn_ref, wvt_ref,
                     qt_ref, kn_ref, kr_ref, vt_ref):
    xb = h_ref[...].astype(BF16)
    tm = h_ref.shape[0]
    scale = MLA_QK ** -0.5 * LOG2E
    cq = _rms_norm(_dot(xb, wcq_ref[...]), gq_ref[...]).astype(BF16)
    ckv = _rms_norm(_dot(xb, wckv_ref[...]), gkv_ref[...]).astype(BF16)
    kn_ref[...] = _dot(ckv, wkn_ref[...]).astype(BF16)
    kr2 = _dot(xb, wkr_ref[...])
    kr = kr2[:, 0:LANES] * cos_ref[...] + kr2[:, LANES:2 * LANES] * sin_ref[...]
    kr_ref[...] = kr.astype(BF16)
    vt = _dot_nt(wvt_ref[...], ckv).astype(BF16)
    for h in range(MLA_HEADS):
        r0 = h * MLA_VROWS
        vt_ref[0, 0, r0:r0 + MLA_V, :] = vt[h * MLA_V:(h + 1) * MLA_V]
        vt_ref[0, 0, r0 + MLA_V:r0 + MLA_VROWS, :] = jnp.ones((BF16_SUBLANES, tm), BF16)
    a = _dot_nt(wqt_ref[...], cq)
    bsw = _dot_nt(wqst_ref[...], cq)
    cos_t = cost_ref[...]
    sin_t = sint_ref[...]
    hq = 2 * LANES
    for h in range(MLA_HEADS):
        r0 = h * hq
        qt_ref[0, 0, r0:r0 + MLA_NOPE, :] = (a[r0:r0 + MLA_NOPE] * scale).astype(BF16)
        rope = a[r0 + MLA_NOPE:r0 + MLA_QK] * cos_t + bsw[h * MLA_ROPE:(h + 1) * MLA_ROPE] * sin_t
        qt_ref[0, 0, r0 + MLA_NOPE:r0 + MLA_QK, :] = (rope * scale).astype(BF16)
        qt_ref[0, 0, r0 + MLA_QK:r0 + hq, :] = jnp.zeros((hq - MLA_QK, qt_ref.shape[3]), BF16)


def _mla_proj(h, b, s, tables, weights):
    m, d = h.shape
    tm = MLA_TK
    nt = s // tm
    row = lambda i: (i, 0)
    pos = lambda i: (i % nt, 0)
    pos_t = lambda i: (0, i % nt)
    tile_t = lambda i: (i // nt, i % nt, 0, 0)
    wide = MLA_HEADS * LANES
    return pl.pallas_call(
        _mla_proj_kernel,
        grid=(m // tm,),
        in_specs=[pl.BlockSpec((tm, d), row), pl.BlockSpec((tm, LANES), pos), pl.BlockSpec((tm, LANES), pos),
                  pl.BlockSpec((MLA_ROPE, tm), pos_t), pl.BlockSpec((MLA_ROPE, tm), pos_t)]
        + [_const_spec(w.shape) for w in weights],
        out_specs=[pl.BlockSpec((1, 1, 2 * wide, tm), tile_t), pl.BlockSpec((tm, wide), row),
                   pl.BlockSpec((tm, LANES), row), pl.BlockSpec((1, 1, MLA_HEADS * MLA_VROWS, tm), tile_t)],
        out_shape=[jax.ShapeDtypeStruct((b, nt, 2 * wide, tm), BF16), jax.ShapeDtypeStruct((m, wide), BF16),
                   jax.ShapeDtypeStruct((m, LANES), BF16),
                   jax.ShapeDtypeStruct((b, nt, MLA_HEADS * MLA_VROWS, tm), BF16)],
        compiler_params=_params(("parallel",)),
        name="mla_proj",
    )(h, *tables, *weights)


def _mla_attn_kernel(qt_ref, qnext_ref, kn_ref, kr_ref, vt_ref, o_ref, m_ref, acc_ref, st_ref, mx_ref):
    qi = pl.program_id(2)
    tq, tk = MLA_TQ, MLA_TK
    assert tq == 2 * tk
    m_ref[...] = jnp.full_like(m_ref, NEG_INF)
    acc_ref[...] = jnp.zeros_like(acc_ref)

    def scores(slot, j, q_ref=qt_ref):
        rows = pl.ds(pl.multiple_of(j * tk, tk), tk)
        kcat = jnp.concatenate([kn_ref[0, rows, :], kr_ref[0, rows, :]], axis=1)
        st_ref[slot, :, 0:tk] = _dot(kcat, q_ref[0, 0])
        st_ref[slot, :, tk:tq] = _dot(kcat, q_ref[0, 1])
        mx_ref[slot] = jnp.max(st_ref[slot], axis=0, keepdims=True)

    def consume(slot, j, first_key_chunk=None):
        st = st_ref[slot]
        if first_key_chunk is None:
            mx = mx_ref[slot]
        else:
            kchunk = first_key_chunk + lax.broadcasted_iota(jnp.int32, (tk, tq), 0) // CHUNK
            qchunk = lax.broadcasted_iota(jnp.int32, (tk, tq), 1) // CHUNK
            st = jnp.where(kchunk <= qchunk, st, NEG_INF)
            mx = jnp.max(st, axis=0, keepdims=True)
        m_prev = m_ref[...]
        m_new = jnp.maximum(m_prev, mx)
        alpha = jnp.exp2(m_prev - m_new)
        p = jnp.exp2(st - m_new).astype(BF16)
        acc_ref[...] = alpha * acc_ref[...] + _dot(vt_ref[0, j], p)
        m_ref[...] = m_new

    @pl.when(qi == 0)
    def _():
        scores(0, 0)

    def pair(i):
        j = 2 * i
        scores(1, j + 1)
        consume(0, j)
        scores(0, j + 2)
        consume(1, j + 1)

    def two_pairs(i, carry):
        pair(2 * i)
        pair(2 * i + 1)
        return carry

    lax.fori_loop(0, qi // 2, two_pairs, 0)

    @pl.when(qi % 2 == 1)
    def _():
        pair(qi - 1)

    scores(1, 2 * qi + 1)
    consume(0, 2 * qi, first_key_chunk=0)
    scores(0, 0, qnext_ref)
    consume(1, 2 * qi + 1, first_key_chunk=tk // CHUNK)

    acc = acc_ref[...]
    o_ref[0] = (acc[0:MLA_V] * (1.0 / acc[MLA_V:MLA_V + 1])).T.astype(BF16)


def _mla_attn(qt, kn, kr, vt):
    b, nt, _, tk = qt.shape
    s = kn.shape[1]
    tq = MLA_TQ
    assert MLA_TK == tk
    last_q = s // tq - 1
    return pl.pallas_call(
        _mla_attn_kernel,
        grid=(b, MLA_HEADS, s // tq),
        in_specs=[
            pl.BlockSpec((1, tq // tk, 2 * LANES, tk), lambda bi, hi, qi: (bi, qi, hi, 0)),
            pl.BlockSpec((1, tq // tk, 2 * LANES, tk), lambda bi, hi, qi: (bi, jnp.minimum(qi + 1, last_q), hi, 0)),
            pl.BlockSpec((1, s, LANES), lambda bi, hi, qi: (bi, 0, hi)),
            pl.BlockSpec((1, s, LANES), lambda bi, hi, qi: (bi, 0, 0)),
            pl.BlockSpec((1, nt, MLA_VROWS, tk), lambda bi, hi, qi: (bi, 0, hi, 0)),
        ],
        out_specs=pl.BlockSpec((1, tq, LANES), lambda bi, hi, qi: (bi, qi, hi)),
        out_shape=jax.ShapeDtypeStruct((b, s, MLA_HEADS * MLA_V), BF16),
        scratch_shapes=[
            pltpu.VMEM((1, tq), F32),
            pltpu.VMEM((MLA_VROWS, tq), F32),
            pltpu.VMEM((2, tk, tq), F32),
            pltpu.VMEM((2, 1, tq), F32),
        ],
        compiler_params=_params(("parallel", "parallel", "arbitrary")),
        name="mla_attn",
    )(qt, qt, kn, kr, vt)


def _swap_halves(w):
    half = w.shape[-1] // 2
    return jnp.concatenate([w[..., half:], w[..., :half]], axis=-1)


def _mla_layer(h, b, s, w_in, q_norm_g, kv_norm_g, w_uq, w_ukv):
    pad = LANES - MLA_ROPE
    w_cq = w_in[:, :MLA_Q_RANK]
    w_ckv = w_in[:, MLA_Q_RANK:MLA_Q_RANK + MLA_KV_RANK]
    w_kr = w_in[:, MLA_Q_RANK + MLA_KV_RANK:]
    pad_cols = lambda w: jnp.pad(w, [(0, 0)] * (w.ndim - 1) + [(0, pad)])
    w_uq = w_uq.reshape(MLA_Q_RANK, MLA_HEADS, MLA_QK)
    w_qt = jnp.pad(w_uq, [(0, 0), (0, 0), (0, 2 * LANES - MLA_QK)]).reshape(MLA_Q_RANK, -1).T
    w_qst = _swap_halves(w_uq[:, :, MLA_NOPE:]).reshape(MLA_Q_RANK, -1).T
    w_ukv = w_ukv.reshape(MLA_KV_RANK, MLA_HEADS, MLA_NOPE + MLA_V)
    w_kn = w_ukv[:, :, :MLA_NOPE].reshape(MLA_KV_RANK, MLA_HEADS * MLA_NOPE)
    w_vt = w_ukv[:, :, MLA_NOPE:].reshape(MLA_KV_RANK, MLA_HEADS * MLA_V).T
    inv = ROPE_BASE ** (-jnp.arange(0, MLA_ROPE, 2, dtype=F32) / MLA_ROPE)
    ang = jnp.arange(s, dtype=F32)[:, None] * inv[None, :]
    cos, sin = jnp.cos(ang), jnp.sin(ang)
    cos2 = jnp.concatenate([cos, cos], axis=1)
    sin2 = jnp.concatenate([-sin, sin], axis=1)
    tables = [jnp.concatenate([cos2, cos2], axis=1), jnp.concatenate([sin2, sin2], axis=1), cos2.T, sin2.T]
    bf = lambda w: w.astype(BF16)
    w_kr2 = jnp.concatenate([pad_cols(w_kr), pad_cols(_swap_halves(w_kr))], axis=1)
    weights = [bf(w_cq), bf(w_ckv), bf(w_kr2),
               q_norm_g.reshape(1, -1), kv_norm_g.reshape(1, -1),
               bf(w_qt), bf(w_qst), bf(w_kn), bf(w_vt)]
    qt, kn, kr, vt = _mla_proj(h, b, s, tables, weights)
    sh = lambda t: t.reshape(b, s, t.shape[-1])
    o = _mla_attn(qt, sh(kn), sh(kr), vt)
    return o.reshape(b * s, MLA_HEADS * MLA_V)


def kernel(x, gla_w_in, gla_w_gate_up, gla_b_gate, gla_norm_g, gla_w_out,
           rpa_w_in, rpa_rel_bias, rpa_w_out,
           mla_w_in, mla_q_norm_g, mla_kv_norm_g, mla_w_uq, mla_w_ukv, mla_w_out,
           ffn_w_in, ffn_w_out, ln_g, ln_b):
    b, s, d = x.shape
    h = x.reshape(b * s, d)
    assert ffn_w_out.shape[1] % FFN_CHUNK == 0
    for i in range(DEPTH):
        m, j = i % N_MIXERS, i // N_MIXERS
        if m == 0:
            o = _gla_layer(h, b, s, gla_w_in[j], gla_w_gate_up[j], gla_b_gate[j], gla_norm_g[j])
            w_o = gla_w_out[j]
        elif m == 1:
            o = _rpa_layer(h, b, s, rpa_w_in[j], rpa_rel_bias[j])
            w_o = rpa_w_out[j]
        else:
            o = _mla_layer(h, b, s, mla_w_in[j], mla_q_norm_g[j], mla_kv_norm_g[j], mla_w_uq[j], mla_w_ukv[j])
            w_o = mla_w_out[j]
        ln = jnp.stack([ln_g[i, 0], ln_b[i, 0], ln_g[i, 1], ln_b[i, 1]])
        h = _post(o, h, w_o.astype(BF16), ln, ffn_w_in[i].astype(BF16), ffn_w_out[i].astype(BF16))
    return h.reshape(b, s, d)
```

```python
import functools

import jax
import jax.numpy as jnp
import numpy as np
from jax import lax
from jax.experimental import pallas as pl
from jax.experimental.pallas import tpu as pltpu

F32 = jnp.float32
BF16 = jnp.bfloat16

LANES = 128
BF16_SUBLANES = 16
VMEM_LIMIT = 56 * 1024 * 1024
LOG2E = 1.4426950408889634

DEPTH = 4
N_MIXERS = 3
CHUNK = 64
DEEPNORM_ALPHA = (2 * DEPTH) ** 0.25
LN_EPS = 1e-5
RMS_EPS = 1e-6
NEG_INF = -1e30

GLA_HEADS = 4
GLA_DK = 128
GLA_DV = 256
GLA_GATE_RANK = 16
GLA_TAU = 16.0
GLA_HK = GLA_HEADS * GLA_DK
GLA_HV = GLA_HEADS * GLA_DV
GLA_BLOCK = 1024

RPA_HEADS = 16
RPA_HEAD_DIM = 64
RPA_LEFT_CHUNKS = 8
RPA_MAX_REL = 128
RPA_GROUP = 2
RPA_GQ = RPA_GROUP * CHUNK
RPA_GK = (RPA_LEFT_CHUNKS + RPA_GROUP) * CHUNK
RPA_TQ = 512
RPA_PAIRS = RPA_HEADS // 2
RPA_VROWS = LANES + BF16_SUBLANES

MLA_HEADS = 8
MLA_NOPE = 128
MLA_ROPE = 64
MLA_V = 128
MLA_Q_RANK = 384
MLA_KV_RANK = 256
MLA_QK = MLA_NOPE + MLA_ROPE
ROPE_BASE = 10000.0
MLA_TQ = 1024
MLA_TK = 512
MLA_VROWS = MLA_V + BF16_SUBLANES

TOKEN_TILE = 512
POST_SUBTILES = 2
FFN_CHUNK = 256


def _params(sem):
    return pltpu.CompilerParams(dimension_semantics=sem, vmem_limit_bytes=VMEM_LIMIT)


def _const_spec(shape):
    nd = len(shape)
    return pl.BlockSpec(shape, lambda *_: (0,) * nd, pipeline_mode=pl.Buffered(1))


def _layer_norm(x, g, b):
    mu = jnp.mean(x, axis=-1, keepdims=True)
    xc = x - mu
    var = jnp.mean(xc * xc, axis=-1, keepdims=True)
    return xc * lax.rsqrt(var + LN_EPS) * g + b


def _dot(a, b):
    return jnp.dot(a, b, preferred_element_type=F32)


def _dot_nt(a, b):
    return lax.dot_general(a, b, (((1,), (1,)), ((), ())), preferred_element_type=F32)


def _dot_tn(a, b):
    return lax.dot_general(a, b, (((0,), (0,)), ((), ())), preferred_element_type=F32)


def _post_kernel(o_ref, h_ref, wo_ref, ln_ref, wgu_ref, wdn_ref, out_ref, xb_ref):
    tm = out_ref.shape[0]
    hidden = wdn_ref.shape[0]
    halves = [slice(r0, r0 + TOKEN_TILE) for r0 in range(0, tm, TOKEN_TILE)]

    n_piece = 8
    pr = TOKEN_TILE // n_piece

    def piece(rows, i):
        return slice(rows.start + i * pr, rows.start + (i + 1) * pr)

    def stage_norm1(rows):
        out_ref[rows, :] = DEEPNORM_ALPHA * h_ref[rows, :] + _dot(o_ref[rows, :], wo_ref[...])

    def anchored_norm(x, g, b, anchor):
        mu = jnp.mean(x, axis=-1, keepdims=True)
        if anchor is not None:
            mu = mu + 0.0 * anchor
        xc = x - mu
        var = jnp.mean(xc * xc, axis=-1, keepdims=True)
        return xc * lax.rsqrt(var + LN_EPS) * g + b

    def norm1_piece(rows, i, anchor=None):
        r = piece(rows, i)
        h1 = anchored_norm(out_ref[r, :], ln_ref[0:1, :], ln_ref[1:2, :], anchor)
        out_ref[r, :] = h1
        xb_ref[r, :] = h1.astype(BF16)

    def norm2_piece(rows, i, anchor=None):
        r = piece(rows, i)
        out_ref[r, :] = anchored_norm(out_ref[r, :], ln_ref[2:3, :], ln_ref[3:4, :], anchor)

    def ffn(rows, after_chunk):
        xb = xb_ref[rows, :]
        acc = None
        for ci, c0 in enumerate(range(0, hidden, FFN_CHUNK)):
            g = _dot(xb, wgu_ref[:, c0:c0 + FFN_CHUNK])
            u = _dot(xb, wgu_ref[:, hidden + c0:hidden + c0 + FFN_CHUNK])
            a = (g * jax.nn.sigmoid(g) * u).astype(BF16)
            part = _dot(a, wdn_ref[c0:c0 + FFN_CHUNK, :])
            acc = part if acc is None else acc + part
            if ci < n_piece:
                after_chunk(ci, part[0:pr, 0:1])
        out_ref[rows, :] = DEEPNORM_ALPHA * out_ref[rows, :] + acc

    assert len(halves) == 2 and hidden // FFN_CHUNK >= n_piece
    first, second = halves
    stage_norm1(first)
    for i in range(n_piece):
        norm1_piece(first, i)
    stage_norm1(second)
    ffn(first, functools.partial(norm1_piece, second))
    ffn(second, functools.partial(norm2_piece, first))
    for i in range(n_piece):
        norm2_piece(second, i)


def _post(o, h, w_o, ln, w_gu, w_dn):
    m, d = h.shape
    tm = POST_SUBTILES * TOKEN_TILE
    row = lambda i: (i, 0)
    return pl.pallas_call(
        _post_kernel,
        grid=(m // tm,),
        in_specs=[
            pl.BlockSpec((tm, o.shape[1]), row),
            pl.BlockSpec((tm, d), row),
            _const_spec(w_o.shape),
            _const_spec(ln.shape),
            _const_spec(w_gu.shape),
            _const_spec(w_dn.shape),
        ],
        out_specs=pl.BlockSpec((tm, d), row),
        out_shape=jax.ShapeDtypeStruct((m, d), F32),
        scratch_shapes=[pltpu.VMEM((tm, d), BF16)],
        compiler_params=_params(("parallel",)),
        name="post_ffn",
    )(o, h, w_o, ln, w_gu, w_dn)


def _split3_bf16(x):
    hi = x.astype(BF16)
    r1 = x - hi.astype(F32)
    mid = r1.astype(BF16)
    lo = (r1 - mid.astype(F32)).astype(BF16)
    return hi, mid, lo


def _gla_proj_kernel(h_ref, wm_ref, wgl_ref, wgu_ref, bg_ref, qk_ref, v_ref, r_ref, la_ref):
    xb = h_ref[...].astype(BF16)
    g_low = _dot(xb, wgl_ref[...])
    z = _dot(g_low.astype(BF16), wgu_ref[...]) + bg_ref[...]
    log_sig = jnp.minimum(z, 0.0) - jnp.log(1.0 + jnp.exp(-jnp.abs(z)))
    la_ref[...] = log_sig / GLA_TAU
    qk_ref[...] = _dot(xb, wm_ref[:, 0:2 * GLA_HK])
    v_ref[...] = _dot(xb, wm_ref[:, 2 * GLA_HK:2 * GLA_HK + GLA_HV]).astype(BF16)
    r_ref[...] = _dot(xb, wm_ref[:, 2 * GLA_HK + GLA_HV:])


def _gla_proj(h, w_main, w_gl, w_gu, b_gate):
    m, d = h.shape
    tm = TOKEN_TILE
    row = lambda i: (i, 0)
    return pl.pallas_call(
        _gla_proj_kernel,
        grid=(m // tm,),
        in_specs=[
            pl.BlockSpec((tm, d), row),
            _const_spec(w_main.shape),
            _const_spec(w_gl.shape),
            _const_spec(w_gu.shape),
            _const_spec(b_gate.shape),
        ],
        out_specs=[
            pl.BlockSpec((tm, 2 * GLA_HK), row),
            pl.BlockSpec((tm, GLA_HV), row),
            pl.BlockSpec((tm, GLA_HV), row),
            pl.BlockSpec((tm, GLA_HK), row),
        ],
        out_shape=[
            jax.ShapeDtypeStruct((m, 2 * GLA_HK), F32),
            jax.ShapeDtypeStruct((m, GLA_HV), BF16),
            jax.ShapeDtypeStruct((m, GLA_HV), F32),
            jax.ShapeDtypeStruct((m, GLA_HK), F32),
        ],
        compiler_params=_params(("parallel",)),
        name="gla_proj",
    )(h, w_main, w_gl, w_gu, b_gate)


def _gla_core_kernel(qk_ref, v_ref, r_ref, la_ref, g_ref, y_ref, state_ref, *, n_chunks):
    slot = 0

    @pl.when(pl.program_id(1) == 0)
    def _():
        state_ref[...] = jnp.zeros_like(state_ref)

    row_i = lax.broadcasted_iota(jnp.int32, (CHUNK, CHUNK), 0)
    col_i = lax.broadcasted_iota(jnp.int32, (CHUNK, CHUNK), 1)
    causal = col_i <= row_i
    tri = causal.astype(BF16)

    def prepare(ci):
        rows = slice(ci * CHUNK, (ci + 1) * CHUNK)
        la = la_ref[slot, rows, :]
        hi, mid, lo = _split3_bf16(la)
        cum = _dot(tri, hi) + _dot(tri, mid) + _dot(tri, lo)
        tot = cum[CHUNK - 1:CHUNK, :]
        q = qk_ref[slot, rows, 0:GLA_HK] * (GLA_DK ** -0.5)
        k = qk_ref[slot, rows, GLA_HK:2 * GLA_HK]
        q_dec = (q * jnp.exp(cum)).astype(BF16)
        k_inv = (k * jnp.exp(-cum)).astype(BF16)
        k_end = (k * jnp.exp(tot - cum)).astype(BF16)
        decay = jnp.exp(tot)
        o_intra = []
        for h in range(GLA_HEADS):
            kc = slice(h * GLA_DK, (h + 1) * GLA_DK)
            vc = slice(h * GLA_DV, (h + 1) * GLA_DV)
            a = jnp.where(causal, _dot_nt(q_dec[:, kc], k_inv[:, kc]), 0.0)
            o_intra.append(_dot(a.astype(BF16), v_ref[slot, rows, vc]))
        return q_dec, k_end, decay, o_intra

    def recur(ci, prepared):
        rows = slice(ci * CHUNK, (ci + 1) * CHUNK)
        q_dec, k_end, decay, o_intra = prepared
        for h in range(GLA_HEADS):
            kc = slice(h * GLA_DK, (h + 1) * GLA_DK)
            vc = slice(h * GLA_DV, (h + 1) * GLA_DV)
            vh = v_ref[slot, rows, vc]
            st = state_ref[h]
            o = o_intra[h] + _dot_nt(q_dec[:, kc], st.astype(BF16))
            state_ref[h] = st * decay[:, kc] + _dot_tn(vh, k_end[:, kc])
            mu = jnp.mean(o, axis=-1, keepdims=True)
            oc = o - mu
            var = jnp.mean(oc * oc, axis=-1, keepdims=True)
            on = oc * lax.rsqrt(var + LN_EPS) * g_ref[:, vc]
            r = r_ref[slot, rows, vc]
            y_ref[slot, rows, vc] = (r * jax.nn.sigmoid(r) * on).astype(BF16)

    prepared = prepare(0)
    for ci in range(n_chunks):
        upcoming = prepare(ci + 1) if ci + 1 < n_chunks else None
        recur(ci, prepared)
        prepared = upcoming


def _gla_core(qk, v, r, la, norm_g, *, block):
    b, s, _ = qk.shape
    blk = lambda bi, ti: (bi, ti, 0)
    return pl.pallas_call(
        functools.partial(_gla_core_kernel, n_chunks=block // CHUNK),
        grid=(b, s // block),
        in_specs=[
            pl.BlockSpec((1, block, 2 * GLA_HK), blk),
            pl.BlockSpec((1, block, GLA_HV), blk),
            pl.BlockSpec((1, block, GLA_HV), blk),
            pl.BlockSpec((1, block, GLA_HK), blk),
            pl.BlockSpec((1, GLA_HV), lambda bi, ti: (0, 0)),
        ],
        out_specs=pl.BlockSpec((1, block, GLA_HV), blk),
        out_shape=jax.ShapeDtypeStruct((b, s, GLA_HV), BF16),
        scratch_shapes=[pltpu.VMEM((GLA_HEADS, GLA_DV, GLA_DK), F32)],
        compiler_params=_params(("parallel", "arbitrary")),
        name="gla_core",
    )(qk, v, r, la, norm_g)


def _gla_layer(h, b, s, w_in, w_gate_up, b_gate, norm_g):
    d = h.shape[1]
    n_main = 2 * GLA_HK + 2 * GLA_HV
    w_main = w_in[:, :n_main].astype(BF16)
    w_gl = jnp.zeros((d, LANES), BF16).at[:, :GLA_GATE_RANK].set(w_in[:, n_main:].astype(BF16))
    w_gu = jnp.zeros((LANES, GLA_HK), BF16).at[:GLA_GATE_RANK, :].set(w_gate_up.astype(BF16))
    qk, v, r, la = _gla_proj(h, w_main, w_gl, w_gu, b_gate.reshape(1, GLA_HK))
    sh = lambda t: t.reshape(b, s, t.shape[-1])
    y = _gla_core(sh(qk), sh(v), sh(r), sh(la), norm_g.reshape(1, GLA_HV), block=GLA_BLOCK)
    return y.reshape(b * s, GLA_HV)


def _rpa_proj_kernel(h_ref, wqt_ref, wk_ref, wvt_ref, qt_ref, k_ref, vt_ref):
    xb = h_ref[...].astype(BF16)
    tm = h_ref.shape[0]
    qt_ref[0] = (_dot_nt(wqt_ref[...], xb) * (RPA_HEAD_DIM ** -0.5 * LOG2E)).astype(BF16)
    k_ref[...] = _dot(xb, wk_ref[...]).astype(BF16)
    vt = _dot_nt(wvt_ref[...], xb).astype(BF16)
    for p in range(RPA_PAIRS):
        r0 = p * RPA_VROWS
        vt_ref[0, r0:r0 + LANES, :] = vt[p * LANES:(p + 1) * LANES]
        vt_ref[0, r0 + LANES:r0 + RPA_VROWS, :] = jnp.ones((BF16_SUBLANES, tm), BF16)


def _rpa_proj(h, b, s, w_qt, w_k, w_vt):
    m, d = h.shape
    tm = RPA_TQ
    nt = s // tm
    row = lambda i: (i, 0)
    col = lambda i: (i // nt, 0, i % nt)
    return pl.pallas_call(
        _rpa_proj_kernel,
        grid=(m // tm,),
        in_specs=[pl.BlockSpec((tm, d), row), _const_spec(w_qt.shape), _const_spec(w_k.shape),
                  _const_spec(w_vt.shape)],
        out_specs=[pl.BlockSpec((1, d, tm), col), pl.BlockSpec((tm, d), row),
                   pl.BlockSpec((1, RPA_PAIRS * RPA_VROWS, tm), col)],
        out_shape=[jax.ShapeDtypeStruct((b, d, s), BF16), jax.ShapeDtypeStruct((m, d), BF16),
                   jax.ShapeDtypeStruct((b, RPA_PAIRS * RPA_VROWS, s), BF16)],
        compiler_params=_params(("parallel",)),
        name="rpa_proj",
    )(h, w_qt, w_k, w_vt)


def _rpa_core_kernel(qt_ref, kp_ref, kc_ref, vtp_ref, vtc_ref, bias_ref, o_ref, kw_ref, vw_ref, st_ref, mx_ref,
                     e_ref):
    t = pl.program_id(1)
    tq = o_ref.shape[1]
    kw_ref[0:tq, :] = kp_ref[0]
    kw_ref[tq:2 * tq, :] = kc_ref[0]
    vw_ref[:, 0:tq] = vtp_ref[0]
    vw_ref[:, tq:2 * tq] = vtc_ref[0]

    first_head = lax.broadcasted_iota(jnp.int32, (LANES, RPA_GQ), 0) < RPA_HEAD_DIM
    key_row = lax.broadcasted_iota(jnp.int32, (RPA_GK, 2 * RPA_GQ), 0)

    def scores(slot, g, p):
        q0 = g * RPA_GQ
        feat = slice(p * LANES, (p + 1) * LANES)
        qt = qt_ref[0, feat, q0:q0 + RPA_GQ]
        zero = jnp.zeros_like(qt)
        q_blk = jnp.concatenate([jnp.where(first_head, qt, zero), jnp.where(first_head, zero, qt)], axis=1)
        st = _dot(kw_ref[q0:q0 + RPA_GK, feat], q_blk) + bias_ref[p]
        st = jnp.where(jnp.logical_and(t == 0, key_row + q0 < tq), NEG_INF, st)
        st_ref[slot] = st
        mx_ref[slot] = jnp.max(st, axis=0, keepdims=True)

    def softmax(slot, g, p):
        e_ref[slot] = jnp.exp2(st_ref[slot] - mx_ref[slot]).astype(BF16)

    def values(slot, g, p):
        q0 = g * RPA_GQ
        feat = slice(p * LANES, (p + 1) * LANES)
        ot = _dot(vw_ref[p * RPA_VROWS:(p + 1) * RPA_VROWS, q0:q0 + RPA_GK], e_ref[slot])
        on = ot[0:LANES, :] * (1.0 / ot[LANES:LANES + 1, :])
        pair_t = jnp.where(first_head, on[:, 0:RPA_GQ], on[:, RPA_GQ:2 * RPA_GQ])
        o_ref[0, q0:q0 + RPA_GQ, feat] = pair_t.T.astype(BF16)

    units = [(g, p) for g in range(tq // RPA_GQ) for p in range(RPA_PAIRS)]
    n = len(units)
    scores(0, *units[0])
    scores(1, *units[1])
    softmax(0, *units[0])
    for i in range(n):
        if i + 2 < n:
            scores(i % 2, *units[i + 2])
        if i + 1 < n:
            softmax((i + 1) % 2, *units[i + 1])
        values(i % 2, *units[i])


def _rpa_core(qt, k, vt, bias_t):
    b, s, d = k.shape
    tq = RPA_TQ
    vrows = vt.shape[1]
    prev = lambda ti: jnp.maximum(ti - 1, 0)
    return pl.pallas_call(
        _rpa_core_kernel,
        grid=(b, s // tq),
        in_specs=[
            pl.BlockSpec((1, d, tq), lambda bi, ti: (bi, 0, ti)),
            pl.BlockSpec((1, tq, d), lambda bi, ti: (bi, prev(ti), 0)),
            pl.BlockSpec((1, tq, d), lambda bi, ti: (bi, ti, 0)),
            pl.BlockSpec((1, vrows, tq), lambda bi, ti: (bi, 0, prev(ti))),
            pl.BlockSpec((1, vrows, tq), lambda bi, ti: (bi, 0, ti)),
            _const_spec(bias_t.shape),
        ],
        out_specs=pl.BlockSpec((1, tq, d), lambda bi, ti: (bi, ti, 0)),
        out_shape=jax.ShapeDtypeStruct((b, s, d), BF16),
        scratch_shapes=[pltpu.VMEM((2 * tq, d), BF16), pltpu.VMEM((vrows, 2 * tq), BF16),
                        pltpu.VMEM((2, RPA_GK, 2 * RPA_GQ), F32), pltpu.VMEM((2, 1, 2 * RPA_GQ), F32),
                        pltpu.VMEM((2, RPA_GK, 2 * RPA_GQ), BF16)],
        compiler_params=_params(("parallel", "parallel")),
        name="rpa_core",
    )(qt, k, k, vt, vt, bias_t)


def _rpa_group_bias_t(rel_bias):
    gq, gk = RPA_GQ, RPA_GK
    span = gq + gk - 1
    diag = np.arange(span)
    rel = np.clip(RPA_LEFT_CHUNKS * CHUNK + gq - 1 - diag, -RPA_MAX_REL, RPA_MAX_REL) + RPA_MAX_REL
    u = jnp.pad(rel_bias[:, rel].astype(F32), ((0, 0), (0, 1)))
    shifted = jnp.tile(u, (1, gq))[:, :gq * span].reshape(-1, gq, span)
    table = shifted[:, :, gq - 1:gq - 1 + gk]
    r = np.arange(gq)[:, None]
    w = np.arange(gk)[None, :]
    band = w - (r // CHUNK) * CHUNK
    in_band = (band >= 0) & (band < (RPA_LEFT_CHUNKS + 1) * CHUNK)
    table = jnp.where(in_band[None], table * LOG2E, NEG_INF)
    table_t = table.transpose(0, 2, 1).reshape(RPA_PAIRS, 2, gk, gq)
    return table_t.transpose(0, 2, 1, 3).reshape(RPA_PAIRS, gk, 2 * gq)


def _rpa_layer(h, b, s, w_in, rel_bias):
    d = h.shape[1]
    w = w_in.astype(BF16)
    qt, k, vt = _rpa_proj(h, b, s, w[:, 0:d].T, w[:, d:2 * d], w[:, 2 * d:3 * d].T)
    o = _rpa_core(qt, k.reshape(b, s, d), vt, _rpa_group_bias_t(rel_bias))
    return o.reshape(b * s, d)


def _rms_norm(x, g):
    return x * lax.rsqrt(jnp.mean(x * x, axis=-1, keepdims=True) + RMS_EPS) * g


def _mla_proj_kernel(h_ref, cos_ref, sin_ref, cost_ref, sint_ref, wcq_ref, wckv_ref, wkr_ref,
                     gq_ref, gkv_ref, wqt_ref, wqst_ref, wkn_ref, wvt_ref,
                     qt_ref, kn_ref, kr_ref, vt_ref):
    xb = h_ref[...].astype(BF16)
    tm = h_ref.shape[0]
    scale = MLA_QK ** -0.5 * LOG2E
    cq = _rms_norm(_dot(xb, wcq_ref[...]), gq_ref[...]).astype(BF16)
    ckv = _rms_norm(_dot(xb, wckv_ref[...]), gkv_ref[...]).astype(BF16)
    kn_ref[...] = _dot(ckv, wkn_ref[...]).astype(BF16)
    kr2 = _dot(xb, wkr_ref[...])
    kr = kr2[:, 0:LANES] * cos_ref[...] + kr2[:, LANES:2 * LANES] * sin_ref[...]
    kr_ref[...] = kr.astype(BF16)
    vt = _dot_nt(wvt_ref[...], ckv).astype(BF16)
    for h in range(MLA_HEADS):
        r0 = h * MLA_VROWS
        vt_ref[0, 0, r0:r0 + MLA_V, :] = vt[h * MLA_V:(h + 1) * MLA_V]
        vt_ref[0, 0, r0 + MLA_V:r0 + MLA_VROWS, :] = jnp.ones((BF16_SUBLANES, tm), BF16)
    a = _dot_nt(wqt_ref[...], cq)
    bsw = _dot_nt(wqst_ref[...], cq)
    cos_t = cost_ref[...]
    sin_t = sint_ref[...]
    hq = 2 * LANES
    for h in range(MLA_HEADS):
        r0 = h * hq
        qt_ref[0, 0, r0:r0 + MLA_NOPE, :] = (a[r0:r0 + MLA_NOPE] * scale).astype(BF16)
        rope = a[r0 + MLA_NOPE:r0 + MLA_QK] * cos_t + bsw[h * MLA_ROPE:(h + 1) * MLA_ROPE] * sin_t
        qt_ref[0, 0, r0 + MLA_NOPE:r0 + MLA_QK, :] = (rope * scale).astype(BF16)
        qt_ref[0, 0, r0 + MLA_QK:r0 + hq, :] = jnp.zeros((hq - MLA_QK, qt_ref.shape[3]), BF16)


def _mla_proj(h, b, s, tables, weights):
    m, d = h.shape
    tm = MLA_TK
    nt = s // tm
    row = lambda i: (i, 0)
    pos = lambda i: (i % nt, 0)
    pos_t = lambda i: (0, i % nt)
    tile_t = lambda i: (i // nt, i % nt, 0, 0)
    wide = MLA_HEADS * LANES
    return pl.pallas_call(
        _mla_proj_kernel,
        grid=(m // tm,),
        in_specs=[pl.BlockSpec((tm, d), row), pl.BlockSpec((tm, LANES), pos), pl.BlockSpec((tm, LANES), pos),
                  pl.BlockSpec((MLA_ROPE, tm), pos_t), pl.BlockSpec((MLA_ROPE, tm), pos_t)]
        + [_const_spec(w.shape) for w in weights],
        out_specs=[pl.BlockSpec((1, 1, 2 * wide, tm), tile_t), pl.BlockSpec((tm, wide), row),
                   pl.BlockSpec((tm, LANES), row), pl.BlockSpec((1, 1, MLA_HEADS * MLA_VROWS, tm), tile_t)],
        out_shape=[jax.ShapeDtypeStruct((b, nt, 2 * wide, tm), BF16), jax.ShapeDtypeStruct((m, wide), BF16),
                   jax.ShapeDtypeStruct((m, LANES), BF16),
                   jax.ShapeDtypeStruct((b, nt, MLA_HEADS * MLA_VROWS, tm), BF16)],
        compiler_params=_params(("parallel",)),
        name="mla_proj",
    )(h, *tables, *weights)


def _mla_attn_kernel(qt_ref, kn_ref, kr_ref, vt_ref, o_ref, m_ref, acc_ref, st_ref, mx_ref):
    qi = pl.program_id(2)
    tq, tk = MLA_TQ, MLA_TK
    assert tq == 2 * tk
    m_ref[...] = jnp.full_like(m_ref, NEG_INF)
    acc_ref[...] = jnp.zeros_like(acc_ref)

    last_q = qt_ref.shape[1] // 2 - 1

    def scores(slot, j, q=qi):
        rows = pl.ds(pl.multiple_of(j * tk, tk), tk)
        kcat = jnp.concatenate([kn_ref[0, rows, :], kr_ref[0, rows, :]], axis=1)
        st_ref[slot, :, 0:tk] = _dot(kcat, qt_ref[0, 2 * q])
        st_ref[slot, :, tk:tq] = _dot(kcat, qt_ref[0, 2 * q + 1])
        mx_ref[slot] = jnp.max(st_ref[slot], axis=0, keepdims=True)

    def consume(slot, j, first_key_chunk=None):
        st = st_ref[slot]
        if first_key_chunk is None:
            mx = mx_ref[slot]
        else:
            kchunk = first_key_chunk + lax.broadcasted_iota(jnp.int32, (tk, tq), 0) // CHUNK
            qchunk = lax.broadcasted_iota(jnp.int32, (tk, tq), 1) // CHUNK
            st = jnp.where(kchunk <= qchunk, st, NEG_INF)
            mx = jnp.max(st, axis=0, keepdims=True)
        m_prev = m_ref[...]
        m_new = jnp.maximum(m_prev, mx)
        alpha = jnp.exp2(m_prev - m_new)
        p = jnp.exp2(st - m_new).astype(BF16)
        acc_ref[...] = alpha * acc_ref[...] + _dot(vt_ref[0, j], p)
        m_ref[...] = m_new

    @pl.when(qi == 0)
    def _():
        scores(0, 0)

    def pair(i):
        j = 2 * i
        scores(1, j + 1)
        consume(0, j)
        scores(0, j + 2)
        consume(1, j + 1)

    def two_pairs(i, carry):
        pair(2 * i)
        pair(2 * i + 1)
        return carry

    lax.fori_loop(0, qi // 2, two_pairs, 0)

    @pl.when(qi % 2 == 1)
    def _():
        pair(qi - 1)

    scores(1, 2 * qi + 1)
    consume(0, 2 * qi, first_key_chunk=0)
    scores(0, 0, jnp.minimum(qi + 1, last_q))
    consume(1, 2 * qi + 1, first_key_chunk=tk // CHUNK)

    acc = acc_ref[...]
    o_ref[0] = (acc[0:MLA_V] * (1.0 / acc[MLA_V:MLA_V + 1])).T.astype(BF16)


def _mla_attn(qt, kn, kr, vt):
    b, nt, _, tk = qt.shape
    s = kn.shape[1]
    tq = MLA_TQ
    assert MLA_TK == tk
    return pl.pallas_call(
        _mla_attn_kernel,
        grid=(b, MLA_HEADS, s // tq),
        in_specs=[
            pl.BlockSpec((1, nt, 2 * LANES, tk), lambda bi, hi, qi: (bi, 0, hi, 0)),
            pl.BlockSpec((1, s, LANES), lambda bi, hi, qi: (bi, 0, hi)),
            pl.BlockSpec((1, s, LANES), lambda bi, hi, qi: (bi, 0, 0)),
            pl.BlockSpec((1, nt, MLA_VROWS, tk), lambda bi, hi, qi: (bi, 0, hi, 0)),
        ],
        out_specs=pl.BlockSpec((1, tq, LANES), lambda bi, hi, qi: (bi, qi, hi)),
        out_shape=jax.ShapeDtypeStruct((b, s, MLA_HEADS * MLA_V), BF16),
        scratch_shapes=[
            pltpu.VMEM((1, tq), F32),
            pltpu.VMEM((MLA_VROWS, tq), F32),
            pltpu.VMEM((2, tk, tq), F32),
            pltpu.VMEM((2, 1, tq), F32),
        ],
        compiler_params=_params(("parallel", "parallel", "arbitrary")),
        name="mla_attn",
    )(qt, kn, kr, vt)


def _swap_halves(w):
    half = w.shape[-1] // 2
    return jnp.concatenate([w[..., half:], w[..., :half]], axis=-1)


def _mla_layer(h, b, s, w_in, q_norm_g, kv_norm_g, w_uq, w_ukv):
    pad = LANES - MLA_ROPE
    w_cq = w_in[:, :MLA_Q_RANK]
    w_ckv = w_in[:, MLA_Q_RANK:MLA_Q_RANK + MLA_KV_RANK]
    w_kr = w_in[:, MLA_Q_RANK + MLA_KV_RANK:]
    pad_cols = lambda w: jnp.pad(w, [(0, 0)] * (w.ndim - 1) + [(0, pad)])
    w_uq = w_uq.reshape(MLA_Q_RANK, MLA_HEADS, MLA_QK)
    w_qt = jnp.pad(w_uq, [(0, 0), (0, 0), (0, 2 * LANES - MLA_QK)]).reshape(MLA_Q_RANK, -1).T
    w_qst = _swap_halves(w_uq[:, :, MLA_NOPE:]).reshape(MLA_Q_RANK, -1).T
    w_ukv = w_ukv.reshape(MLA_KV_RANK, MLA_HEADS, MLA_NOPE + MLA_V)
    w_kn = w_ukv[:, :, :MLA_NOPE].reshape(MLA_KV_RANK, MLA_HEADS * MLA_NOPE)
    w_vt = w_ukv[:, :, MLA_NOPE:].reshape(MLA_KV_RANK, MLA_HEADS * MLA_V).T
    inv = ROPE_BASE ** (-jnp.arange(0, MLA_ROPE, 2, dtype=F32) / MLA_ROPE)
    ang = jnp.arange(s, dtype=F32)[:, None] * inv[None, :]
    cos, sin = jnp.cos(ang), jnp.sin(ang)
    cos2 = jnp.concatenate([cos, cos], axis=1)
    sin2 = jnp.concatenate([-sin, sin], axis=1)
    tables = [jnp.concatenate([cos2, cos2], axis=1), jnp.concatenate([sin2, sin2], axis=1), cos2.T, sin2.T]
    bf = lambda w: w.astype(BF16)
    w_kr2 = jnp.concatenate([pad_cols(w_kr), pad_cols(_swap_halves(w_kr))], axis=1)
    weights = [bf(w_cq), bf(w_ckv), bf(w_kr2),
               q_norm_g.reshape(1, -1), kv_norm_g.reshape(1, -1),
               bf(w_qt), bf(w_qst), bf(w_kn), bf(w_vt)]
    qt, kn, kr, vt = _mla_proj(h, b, s, tables, weights)
    sh = lambda t: t.reshape(b, s, t.shape[-1])
    o = _mla_attn(qt, sh(kn), sh(kr), vt)
    return o.reshape(b * s, MLA_HEADS * MLA_V)


def kernel(x, gla_w_in, gla_w_gate_up, gla_b_gate, gla_norm_g, gla_w_out,
           rpa_w_in, rpa_rel_bias, rpa_w_out,
           mla_w_in, mla_q_norm_g, mla_kv_norm_g, mla_w_uq, mla_w_ukv, mla_w_out,
           ffn_w_in, ffn_w_out, ln_g, ln_b):
    b, s, d = x.shape
    h = x.reshape(b * s, d)
    assert ffn_w_out.shape[1] % FFN_CHUNK == 0
    for i in range(DEPTH):
        m, j = i % N_MIXERS, i // N_MIXERS
        if m == 0:
            o = _gla_layer(h, b, s, gla_w_in[j], gla_w_gate_up[j], gla_b_gate[j], gla_norm_g[j])
            w_o = gla_w_out[j]
        elif m == 1:
            o = _rpa_layer(h, b, s, rpa_w_in[j], rpa_rel_bias[j])
            w_o = rpa_w_out[j]
        else:
            o = _mla_layer(h, b, s, mla_w_in[j], mla_q_norm_g[j], mla_kv_norm_g[j], mla_w_uq[j], mla_w_ukv[j])
            w_o = mla_w_out[j]
        ln = jnp.stack([ln_g[i, 0], ln_b[i, 0], ln_g[i, 1], ln_b[i, 1]])
        h = _post(o, h, w_o.astype(BF16), ln, ffn_w_in[i].astype(BF16), ffn_w_out[i].astype(BF16))
    return h.reshape(b, s, d)
```
